```python
import math
import jax, jax.numpy as jnp
from jax import lax
import numpy as np

D_MODEL = 2048
BATCH = 4
SEQ = 2048
DEPTH = 4
DEC_BATCH = 8
DEC_SEQ = 1
PAST_LEN = 16384
PAGE_SIZE = 128

N_EVEN = (DEPTH + 1) // 2
N_ODD = DEPTH // 2
CONV_W = 4
W_A = D_MODEL // 2
RG_BLOCKS = 8
RG_BS = W_A // RG_BLOCKS
RG_C = 8.0
DA_HEADS = 4
DA_HD = 128
DA_VD = 2 * DA_HD
DA_QW = DA_HEADS * 2 * DA_HD
W_B = DA_HEADS * DA_VD
ROPE_THETA = 10000.0
Q_BLOCK = 128
W_C = D_MODEL
ML_HEADS = 4
ML_HD = W_C // ML_HEADS
ML_QKV_BS = 4
ML_CHUNK = 64
N_EXPERTS = 16
N_GROUPS = 4
EXP_PER_GROUP = N_EXPERTS // N_GROUPS
TOP_K = 2
D_EXPERT = D_MODEL // 2
MOE_BLOCK = 128
ALPHA = (2.0 * DEPTH) ** 0.25
BETA = (8.0 * DEPTH) ** -0.25
LN_EPS = 1e-5
EVEN_IN = 2 * W_A + 2 * DA_QW + W_B
ODD_IN = 2 * W_C + 2 * ML_HEADS

kernel_name = 'hybrid_rglru_diffattn_mlstm_moe_step'

F32 = jnp.float32


def _layer_norm(x, g, b):
    xf = x.astype(F32)
    mu = jnp.mean(xf, -1, keepdims=True)
    var = jnp.mean(jnp.square(xf - mu), -1, keepdims=True)
    return ((xf - mu) * lax.rsqrt(var + LN_EPS) * g.astype(F32) + b.astype(F32)).astype(x.dtype)


def _causal_conv(x, buf, w, b):
    t = x.shape[1]
    xp = jnp.concatenate([buf.astype(x.dtype), x], axis=1)
    y = b
    for j in range(CONV_W):
        y = y + w[j] * xp[:, j:j + t]
    return y, xp[:, t:]


def _block_diag(x, w):
    nb, bs, bo = w.shape
    xs = x.reshape(x.shape[:-1] + (nb, bs))
    y = jnp.einsum('...nc,ncd->...nd', xs, w)
    return y.reshape(x.shape[:-1] + (nb * bo,))


def _lin_combine(c1, c2):
    a1, b1 = c1
    a2, b2 = c2
    return a1 * a2, a2 * b1 + b2


def _rglru(xa, ga, conv_buf, h0, conv_w, conv_b, w_r, b_r, w_i, b_i, lam):
    xc, new_buf = _causal_conv(xa, conv_buf, conv_w, conv_b)
    r = jax.nn.sigmoid(_block_diag(xc, w_r) + b_r).astype(F32)
    i = jax.nn.sigmoid(_block_diag(xc, w_i) + b_i)
    log_a = -RG_C * jax.nn.softplus(-lam.astype(F32)) * r
    a = jnp.exp(log_a)
    bt = jnp.sqrt(-jnp.expm1(2.0 * log_a)) * (i * xc).astype(F32)
    bt = bt.at[:, 0].add(a[:, 0] * h0.astype(F32))
    _, h = lax.associative_scan(_lin_combine, (a, bt), axis=1)
    y = h.astype(xa.dtype) * jax.nn.gelu(ga)
    return y, new_buf, h[:, -1].astype(xa.dtype)


def _rope(x, pos):
    half = DA_HD // 2
    inv = ROPE_THETA ** (-jnp.arange(half, dtype=F32) / half)
    ang = pos.astype(F32)[:, None] * inv[None, :]
    cos = jnp.cos(ang)[:, None, None, :]
    sin = jnp.sin(ang)[:, None, None, :]
    xf = x.astype(F32)
    x1, x2 = xf[..., :half], xf[..., half:]
    return jnp.concatenate([x1 * cos - x2 * sin, x2 * cos + x1 * sin], -1).astype(x.dtype)


def _diff_attn_block(q, q_pos, k, v, k_pos, lam):
    s = jnp.einsum('bqhcd,bkhcd->bhcqk', q.astype(F32), k) * (DA_HD ** -0.5)
    mask = k_pos[None, :] <= q_pos[:, None]
    s = jnp.where(mask, s, -jnp.inf)
    p = jax.nn.softmax(s, axis=-1)
    a = p[:, :, 0] - lam * p[:, :, 1]
    return jnp.einsum('bhqk,bkhe->bqhe', a, v)


def _diff_attention(q, k, v, q_pos, k_pos, lam):
    b, t = q.shape[:2]
    kf = k.astype(F32)
    vf = v.astype(F32)
    if t > Q_BLOCK and t % Q_BLOCK == 0:
        nb = t // Q_BLOCK
        qb = jnp.moveaxis(q.reshape((b, nb, Q_BLOCK) + q.shape[2:]), 1, 0)
        pb = q_pos.reshape(nb, Q_BLOCK)
        o = lax.map(lambda a: _diff_attn_block(a[0], a[1], kf, vf, k_pos, lam), (qb, pb))
        return jnp.moveaxis(o, 0, 1).reshape(b, t, DA_HEADS, DA_VD)
    return _diff_attn_block(q, q_pos, kf, vf, k_pos, lam)


def _even_mixer(x, conv_buf, h0, past_k, past_v, start, w_in, w_out, conv_w, conv_b, w_r, b_r,
                w_i, b_i, rg_lam, lq1, lk1, lq2, lk2, sub_g, lam_init):
    b, t, _ = x.shape
    u = x @ w_in
    xa, ga, q, k, v = jnp.split(u, [W_A, 2 * W_A, 2 * W_A + DA_QW, 2 * W_A + 2 * DA_QW], axis=-1)
    y_a, conv_new, h_new = _rglru(xa, ga, conv_buf, h0, conv_w, conv_b, w_r, b_r, w_i, b_i, rg_lam)
    pos = start + jnp.arange(t, dtype=jnp.int32)
    q = _rope(q.reshape(b, t, DA_HEADS, 2, DA_HD), pos)
    k = _rope(k.reshape(b, t, DA_HEADS, 2, DA_HD), pos)
    v = v.reshape(b, t, DA_HEADS, DA_VD)
    k_rows = k.reshape(b, t, DA_HEADS, 2 * DA_HD)
    if past_k is None:
        k_all, v_all = k, v
    else:
        k_all = jnp.concatenate([past_k.astype(x.dtype), k_rows], axis=1)
        k_all = k_all.reshape(b, -1, DA_HEADS, 2, DA_HD)
        v_all = jnp.concatenate([past_v.astype(x.dtype), v], axis=1)
    k_pos = jnp.arange(start + t, dtype=jnp.int32)
    lam = (jnp.exp(jnp.sum(lq1.astype(F32) * lk1.astype(F32)))
           - jnp.exp(jnp.sum(lq2.astype(F32) * lk2.astype(F32))) + lam_init)
    o = _diff_attention(q, k_all, v_all, pos, k_pos, lam)
    o = o * lax.rsqrt(jnp.mean(jnp.square(o), -1, keepdims=True) + LN_EPS) * sub_g.astype(F32)
    o = (o * (1.0 - lam_init)).reshape(b, t, W_B).astype(x.dtype)
    out = jnp.concatenate([y_a, o], axis=-1) @ w_out
    return out, k_rows, v, h_new, conv_new


def _mlstm_chunk(carry, inp):
    c0, n0, m0 = carry
    q, k, v, logi, logf = inp
    L = q.shape[1]
    F = jnp.cumsum(logf, axis=1).transpose(0, 2, 1)
    li = logi.transpose(0, 2, 1)
    causal = jnp.tril(jnp.ones((L, L), dtype=bool))
    dm = jnp.where(causal, F[..., :, None] - F[..., None, :] + li[..., None, :], -jnp.inf)
    bt = F + m0[..., None]
    m = jnp.maximum(bt, jnp.max(dm, -1))
    wd = jnp.exp(dm - m[..., None])
    qf, kf, vf = q.astype(F32), k.astype(F32), v.astype(F32)
    s = jnp.einsum('blhd,bshd->bhls', qf, kf) * wd
    inter = jnp.exp(bt - m)
    num = jnp.einsum('bhls,bshd->bhld', s, vf) + inter[..., None] * jnp.einsum('blhd,bhde->bhle', qf, c0)
    den = jnp.sum(s, -1) + inter * jnp.einsum('blhd,bhd->bhl', qf, n0)
    h = num / jnp.maximum(jnp.abs(den), jnp.exp(-m))[..., None]
    m_last = m[..., -1]
    wk = jnp.exp(F[..., -1:] - F + li - m_last[..., None])
    decay = jnp.exp(F[..., -1] + m0 - m_last)
    c_new = decay[..., None, None] * c0 + jnp.einsum('bhs,bshd,bshe->bhde', wk, kf, vf)
    n_new = decay[..., None] * n0 + jnp.einsum('bhs,bshd->bhd', wk, kf)
    return (c_new, n_new, m_last), h.transpose(0, 2, 1, 3)


def _to_chunks(a, nc, L):
    return jnp.moveaxis(a.reshape((a.shape[0], nc, L) + a.shape[2:]), 1, 0)


def _mlstm_mixer(x, conv_buf, c0, n0, m0, w_in, w_out, conv_w, conv_b, w_q, w_k, w_v, b_i, b_f, gn_g, skip):
    b, t, _ = x.shape
    u = x @ w_in
    xm, z, ig, fg = jnp.split(u, [W_C, 2 * W_C, 2 * W_C + ML_HEADS], axis=-1)
    xc, conv_new = _causal_conv(xm, conv_buf, conv_w, conv_b)
    xc = jax.nn.silu(xc)
    q = _block_diag(xc, w_q).reshape(b, t, ML_HEADS, ML_HD)
    k = _block_diag(xc, w_k).reshape(b, t, ML_HEADS, ML_HD) * (ML_HD ** -0.5)
    v = _block_diag(xm, w_v).reshape(b, t, ML_HEADS, ML_HD)
    logi = (ig + b_i).astype(F32)
    logf = jax.nn.log_sigmoid((fg + b_f).astype(F32))
    L = ML_CHUNK if t % ML_CHUNK == 0 else t
    nc = t // L
    xs = tuple(_to_chunks(a, nc, L) for a in (q, k, v, logi, logf))
    carry0 = (c0.astype(F32), n0.astype(F32), m0.astype(F32))
    (c_new, n_new, m_new), h = lax.scan(_mlstm_chunk, carry0, xs)
    h = jnp.moveaxis(h, 0, 1).reshape(b, t, ML_HEADS, ML_HD)
    mu = jnp.mean(h, -1, keepdims=True)
    var = jnp.mean(jnp.square(h - mu), -1, keepdims=True)
    h = ((h - mu) * lax.rsqrt(var + LN_EPS)).reshape(b, t, W_C) * gn_g.astype(F32)
    out = jax.nn.sigmoid(z.astype(F32)) * (h + skip.astype(F32) * xc.astype(F32))
    out = out.astype(x.dtype) @ w_out
    return out, conv_new, c_new.astype(x.dtype), n_new.astype(x.dtype), m_new.astype(x.dtype)


def _route(xf, w_r, b_r):
    n = xf.shape[0]
    s = jax.nn.sigmoid(xf.astype(F32) @ w_r.astype(F32))
    sb = s + b_r.astype(F32)
    grp = sb.reshape(n, N_GROUPS, EXP_PER_GROUP)
    gscore = jnp.sum(lax.top_k(grp, 2)[0], -1)
    g = jnp.argmax(gscore, -1)
    in_g = jnp.take_along_axis(grp, g[:, None, None], axis=1)[:, 0]
    _, loc = lax.top_k(in_g, TOP_K)
    idx = g[:, None] * EXP_PER_GROUP + loc
    w = jnp.take_along_axis(s, idx, -1)
    return idx.astype(jnp.int32), w / jnp.sum(w, -1, keepdims=True)


def _moe(x, w_r, b_r, w1, w3, w2):
    b, t, d = x.shape
    xf = x.reshape(-1, d)
    n = xf.shape[0]
    idx, gate = _route(xf, w_r, b_r)
    na = n * TOP_K
    e_flat = idx.reshape(na)
    tok = jnp.repeat(jnp.arange(n, dtype=jnp.int32), TOP_K)
    order = jnp.argsort(e_flat)
    e_s, tok_s, g_s = e_flat[order], tok[order], gate.reshape(na)[order]
    counts = jnp.zeros((N_EXPERTS,), jnp.int32).at[e_flat].add(1)
    padded = (counts + MOE_BLOCK - 1) // MOE_BLOCK * MOE_BLOCK
    pad_end = jnp.cumsum(padded)
    pad_start = pad_end - padded
    start = jnp.cumsum(counts) - counts
    dest = pad_start[e_s] + jnp.arange(na, dtype=jnp.int32) - start[e_s]
    n_blocks = -(-(na + N_EXPERTS * (MOE_BLOCK - 1)) // MOE_BLOCK)
    rows = n_blocks * MOE_BLOCK
    row_tok = jnp.full((rows,), n, jnp.int32).at[dest].set(tok_s)
    row_gate = jnp.zeros((rows,), F32).at[dest].set(g_s)
    blk_e = jnp.minimum(jnp.searchsorted(pad_end, jnp.arange(n_blocks, dtype=jnp.int32) * MOE_BLOCK,
                                         side='right'), N_EXPERTS - 1)
    xpad = jnp.concatenate([xf, jnp.zeros((1, d), xf.dtype)], axis=0)
    xb = xpad[row_tok].reshape(n_blocks, MOE_BLOCK, d)

    def expert_block(a):
        xblk, e = a
        hid = jax.nn.silu(xblk @ w1[e]) * (xblk @ w3[e])
        return hid @ w2[e]

    yb = lax.map(expert_block, (xb, blk_e)).reshape(rows, d)
    y = jnp.zeros((n + 1, d), F32).at[row_tok].add(yb.astype(F32) * row_gate[:, None])
    return y[:n].astype(x.dtype).reshape(b, t, d)


def setup_inputs(seed: int = 0) -> dict:
    key = jax.random.key(seed)
    ks = iter(jax.random.split(key, 64))

    def nrm(shape, scale):
        return jax.random.normal(next(ks), shape, F32) * scale

    n_pages = PAST_LEN // PAGE_SIZE
    n_pool = (5 * DEC_BATCH * n_pages) // 4
    page_table = jax.random.permutation(next(ks), n_pool)[:DEC_BATCH * n_pages]
    page_table = page_table.reshape(DEC_BATCH, n_pages).astype(jnp.int32)
    u = jax.random.uniform(next(ks), (N_EVEN, W_A), F32, 0.9, 0.999)
    sa = u ** (1.0 / RG_C)
    rg_lambda = jnp.log(sa) - jnp.log1p(-sa)
    return {
        'x_prompt': nrm((BATCH, SEQ, D_MODEL), 1.0),
        'x_sample': nrm((DEC_BATCH, DEC_SEQ, D_MODEL), 1.0),
        'cache_k': nrm((N_EVEN, n_pool, PAGE_SIZE, DA_HEADS, 2 * DA_HD), 1.0),
        'cache_v': nrm((N_EVEN, n_pool, PAGE_SIZE, DA_HEADS, DA_VD), 1.0),
        'state_rglru_h': nrm((N_EVEN, DEC_BATCH, W_A), 0.5),
        'state_rglru_conv': nrm((N_EVEN, DEC_BATCH, CONV_W - 1, W_A), 1.0),
        'state_mlstm_c': nrm((N_ODD, DEC_BATCH, ML_HEADS, ML_HD, ML_HD), 0.1),
        'state_mlstm_n': nrm((N_ODD, DEC_BATCH, ML_HEADS, ML_HD), 1.0),
        'state_mlstm_m': nrm((N_ODD, DEC_BATCH, ML_HEADS), 1.0),
        'state_mlstm_conv': nrm((N_ODD, DEC_BATCH, CONV_W - 1, W_C), 1.0),
        'page_table': page_table,
        'ln1_g': 1.0 + nrm((DEPTH, D_MODEL), 0.02),
        'ln1_b': nrm((DEPTH, D_MODEL), 0.02),
        'ln2_g': 1.0 + nrm((DEPTH, D_MODEL), 0.02),
        'ln2_b': nrm((DEPTH, D_MODEL), 0.02),
        'w_in_even': nrm((N_EVEN, D_MODEL, EVEN_IN), D_MODEL ** -0.5),
        'w_out_even': nrm((N_EVEN, W_A + W_B, D_MODEL), BETA * (W_A + W_B) ** -0.5),
        'rg_conv_w': nrm((N_EVEN, CONV_W, W_A), 0.5),
        'rg_conv_b': nrm((N_EVEN, W_A), 0.02),
        'rg_w_r': nrm((N_EVEN, RG_BLOCKS, RG_BS, RG_BS), RG_BS ** -0.5),
        'rg_b_r': nrm((N_EVEN, W_A), 0.1),
        'rg_w_i': nrm((N_EVEN, RG_BLOCKS, RG_BS, RG_BS), RG_BS ** -0.5),
        'rg_b_i': nrm((N_EVEN, W_A), 0.1),
        'rg_lambda': rg_lambda,
        'da_lam_q1': nrm((N_EVEN, DA_HD), 0.1),
        'da_lam_k1': nrm((N_EVEN, DA_HD), 0.1),
        'da_lam_q2': nrm((N_EVEN, DA_HD), 0.1),
        'da_lam_k2': nrm((N_EVEN, DA_HD), 0.1),
        'da_subln_g': 1.0 + nrm((N_EVEN, DA_VD), 0.02),
        'w_in_odd': nrm((N_ODD, D_MODEL, ODD_IN), D_MODEL ** -0.5),
        'w_out_odd': nrm((N_ODD, W_C, D_MODEL), BETA * W_C ** -0.5),
        'ml_conv_w': nrm((N_ODD, CONV_W, W_C), 0.5),
        'ml_conv_b': nrm((N_ODD, W_C), 0.02),
        'ml_w_q': nrm((N_ODD, W_C // ML_QKV_BS, ML_QKV_BS, ML_QKV_BS), ML_QKV_BS ** -0.5),
        'ml_w_k': nrm((N_ODD, W_C // ML_QKV_BS, ML_QKV_BS, ML_QKV_BS), ML_QKV_BS ** -0.5),
        'ml_w_v': nrm((N_ODD, W_C // ML_QKV_BS, ML_QKV_BS, ML_QKV_BS), ML_QKV_BS ** -0.5),
        'ml_b_i': nrm((N_ODD, ML_HEADS), 0.1),
        'ml_b_f': jnp.tile(jnp.linspace(3.0, 6.0, ML_HEADS, dtype=F32), (N_ODD, 1)) + nrm((N_ODD, ML_HEADS), 0.1),
        'ml_gn_g': 1.0 + nrm((N_ODD, W_C), 0.02),
        'ml_skip': 1.0 + nrm((N_ODD, W_C), 0.1),
        'w_router': nrm((D_MODEL, N_EXPERTS), D_MODEL ** -0.5),
        'b_router': nrm((N_EXPERTS,), 0.01),
        'w_e1': nrm((DEPTH, N_EXPERTS, D_MODEL, D_EXPERT), D_MODEL ** -0.5),
        'w_e3': nrm((DEPTH, N_EXPERTS, D_MODEL, D_EXPERT), D_MODEL ** -0.5),
        'w_e2': nrm((DEPTH, N_EXPERTS, D_EXPERT, D_MODEL), BETA * D_EXPERT ** -0.5),
    }


def reference(x_prompt, x_sample, cache_k, cache_v, state_rglru_h, state_rglru_conv, state_mlstm_c,
              state_mlstm_n, state_mlstm_m, state_mlstm_conv, page_table, ln1_g, ln1_b, ln2_g, ln2_b,
              w_in_even, w_out_even, rg_conv_w, rg_conv_b, rg_w_r, rg_b_r, rg_w_i, rg_b_i, rg_lambda,
              da_lam_q1, da_lam_k1, da_lam_q2, da_lam_k2, da_subln_g, w_in_odd, w_out_odd, ml_conv_w,
              ml_conv_b, ml_w_q, ml_w_k, ml_w_v, ml_b_i, ml_b_f, ml_gn_g, ml_skip, w_router, b_router,
              w_e1, w_e3, w_e2):
    x_p, x_s = x_prompt, x_sample
    bp = x_p.shape[0]
    bs = x_s.shape[0]
    kp_l, vp_l, ks_l, vs_l, hp_l, hs_l, rcp_l, rcs_l = [], [], [], [], [], [], [], []
    cp_l, cs_l, np_l, ns_l, mp_l, ms_l, mcp_l, mcs_l = [], [], [], [], [], [], [], []
    for l in range(DEPTH):
        if l % 2 == 0:
            e = l // 2
            lam_init = 0.8 - 0.6 * math.exp(-0.3 * l)
            wts = (w_in_even[e], w_out_even[e], rg_conv_w[e], rg_conv_b[e], rg_w_r[e], rg_b_r[e],
                   rg_w_i[e], rg_b_i[e], rg_lambda[e], da_lam_q1[e], da_lam_k1[e], da_lam_q2[e],
                   da_lam_k2[e], da_subln_g[e], lam_init)
            mix_p, k_p, v_p, h_p, rc_p = _even_mixer(
                x_p, jnp.zeros((bp, CONV_W - 1, W_A), x_p.dtype), jnp.zeros((bp, W_A), x_p.dtype),
                None, None, 0, *wts)
            past_k = cache_k[e, page_table].reshape(bs, -1, DA_HEADS, 2 * DA_HD)
            past_v = cache_v[e, page_table].reshape(bs, -1, DA_HEADS, DA_VD)
            mix_s, k_s, v_s, h_s, rc_s = _even_mixer(
                x_s, state_rglru_conv[e], state_rglru_h[e], past_k, past_v, PAST_LEN, *wts)
            kp_l.append(k_p); vp_l.append(v_p); ks_l.append(k_s); vs_l.append(v_s)
            hp_l.append(h_p); hs_l.append(h_s); rcp_l.append(rc_p); rcs_l.append(rc_s)
        else:
            o = l // 2
            wts = (w_in_odd[o], w_out_odd[o], ml_conv_w[o], ml_conv_b[o], ml_w_q[o], ml_w_k[o],
                   ml_w_v[o], ml_b_i[o], ml_b_f[o], ml_gn_g[o], ml_skip[o])
            mix_p, mc_p, c_p, n_p, m_p = _mlstm_mixer(
                x_p, jnp.zeros((bp, CONV_W - 1, W_C), x_p.dtype),
                jnp.zeros((bp, ML_HEADS, ML_HD, ML_HD), F32), jnp.zeros((bp, ML_HEADS, ML_HD), F32),
                jnp.zeros((bp, ML_HEADS), F32), *wts)
            mix_s, mc_s, c_s, n_s, m_s = _mlstm_mixer(
                x_s, state_mlstm_conv[o], state_mlstm_c[o], state_mlstm_n[o], state_mlstm_m[o], *wts)
            cp_l.append(c_p); cs_l.append(c_s); np_l.append(n_p); ns_l.append(n_s)
            mp_l.append(m_p); ms_l.append(m_s); mcp_l.append(mc_p); mcs_l.append(mc_s)
        x_p = _layer_norm(ALPHA * x_p + mix_p, ln1_g[l], ln1_b[l])
        x_s = _layer_norm(ALPHA * x_s + mix_s, ln1_g[l], ln1_b[l])
        x_p = _layer_norm(ALPHA * x_p + _moe(x_p, w_router, b_router, w_e1[l], w_e3[l], w_e2[l]), ln2_g[l], ln2_b[l])
        x_s = _layer_norm(ALPHA * x_s + _moe(x_s, w_router, b_router, w_e1[l], w_e3[l], w_e2[l]), ln2_g[l], ln2_b[l])
    return (x_p, x_s,
            jnp.stack(kp_l), jnp.stack(vp_l), jnp.stack(ks_l), jnp.stack(vs_l),
            jnp.stack(hp_l), jnp.stack(hs_l), jnp.stack(rcp_l), jnp.stack(rcs_l),
            jnp.stack(cp_l), jnp.stack(cs_l), jnp.stack(np_l), jnp.stack(ns_l),
            jnp.stack(mp_l), jnp.stack(ms_l), jnp.stack(mcp_l), jnp.stack(mcs_l))
```

```python
import functools
import math

import jax
import jax.numpy as jnp
from jax import lax
from jax.experimental import pallas as pl
from jax.experimental.pallas import tpu as pltpu

F32 = jnp.float32
I32 = jnp.int32
BF16 = jnp.bfloat16

V7X_LANES = 128
V7X_SUBLANES = 8
VMEM_LIMIT_BYTES = 60 * 1024 * 1024

CONV_W = 4
RG_C = 8.0
ROPE_THETA = 10000.0
N_GROUPS = 4
TOP_K = 2
LN_EPS = 1e-5
NEG_BIG = -1e30

PRECISE_PASSES = 3
LAST_LAYER_PASSES = 1

NN = (((1,), (0,)), ((), ()))
NT = (((1,), (1,)), ((), ()))
TN = (((0,), (0,)), ((), ()))


def _cparams(sem):
    return pltpu.CompilerParams(dimension_semantics=sem, vmem_limit_bytes=VMEM_LIMIT_BYTES)


def _split(x):
    hi = x.astype(BF16)
    lo = (x - hi.astype(F32)).astype(BF16)
    return hi, lo


def _dg(a, b, dims):
    return lax.dot_general(a, b, dims, preferred_element_type=F32)


def _mxdot(a, b, passes, dims=NN):
    if passes == 1:
        return _dg(a.astype(BF16), b.astype(BF16), dims)
    ah, al = _split(a)
    bh, bl = _split(b)
    return _dg(ah, bh, dims) + (_dg(al, bh, dims) + _dg(ah, bl, dims))


def _mxdot_pre(a, bh, bl, passes, dims=NN):
    if passes == 1:
        return _dg(a.astype(BF16), bh, dims)
    ah, al = _split(a)
    return _dg(ah, bh, dims) + (_dg(al, bh, dims) + _dg(ah, bl, dims))


def _store_split(w, wh_ref, wl_ref, passes):
    if passes == 1:
        wh_ref[...] = w.astype(BF16)
    else:
        h, l = _split(w)
        wh_ref[...] = h
        wl_ref[...] = l


def _r3(a):
    return a.reshape(a.shape[0], 1, a.shape[1])


def _lo_shape(shape, passes):
    return shape if passes > 1 else (V7X_SUBLANES, V7X_LANES)


def _mm_kernel(*refs, k_spans, passes):
    n_x = len(k_spans)
    x_refs, w_ref, o_ref, wh, wl = refs[:n_x], refs[n_x], refs[n_x + 1], refs[n_x + 2], refs[n_x + 3]

    @pl.when(pl.program_id(1) == 0)
    def _():
        _store_split(w_ref[0], wh, wl, passes)

    acc = None
    for x_ref, (k0, kn) in zip(x_refs, k_spans):
        bl = wl[k0:k0 + kn, :] if passes > 1 else None
        p = _mxdot_pre(x_ref[...], wh[k0:k0 + kn, :], bl, passes)
        acc = p if acc is None else acc + p
    o_ref[...] = acc


def _mm(xs, w, layer, col0, ncols, tm, tn, passes):
    m = xs[0].shape[0]
    k = w.shape[1]
    spans, k0 = [], 0
    for x in xs:
        spans.append((k0, x.shape[1]))
        k0 += x.shape[1]
    assert k0 == k and m % tm == 0 and ncols % tn == 0 and col0 % tn == 0
    cb0 = col0 // tn
    in_specs = [pl.BlockSpec((tm, x.shape[1]), lambda j, i: (i, 0)) for x in xs]
    in_specs.append(pl.BlockSpec((1, k, tn), lambda j, i: (layer, 0, j + cb0)))
    return pl.pallas_call(
        functools.partial(_mm_kernel, k_spans=tuple(spans), passes=passes),
        out_shape=jax.ShapeDtypeStruct((m, ncols), F32),
        grid=(ncols // tn, m // tm),
        in_specs=in_specs,
        out_specs=pl.BlockSpec((tm, tn), lambda j, i: (i, j)),
        scratch_shapes=[pltpu.VMEM((k, tn), BF16), pltpu.VMEM(_lo_shape((k, tn), passes), BF16)],
        compiler_params=_cparams(("arbitrary", "arbitrary")),
        name="mm",
    )(*xs, w)


def _layer_norm_rows(y, g, b):
    mu = jnp.mean(y, axis=-1, keepdims=True)
    yc = y - mu
    var = jnp.mean(yc * yc, axis=-1, keepdims=True)
    return yc * lax.rsqrt(var + LN_EPS) * g + b


def _ln_add_kernel(x_ref, mix_ref, g_ref, b_ref, o_ref, *, alpha):
    y = alpha * x_ref[...] + mix_ref[...]
    o_ref[...] = _layer_norm_rows(y, g_ref[...], b_ref[...])


def _ln_add(x, mix, g, b, layer, alpha, tm):
    m, d = x.shape
    row = pl.BlockSpec((tm, d), lambda i: (i, 0))
    vec = pl.BlockSpec((None, 1, d), lambda i: (layer, 0, 0))
    return pl.pallas_call(
        functools.partial(_ln_add_kernel, alpha=alpha),
        out_shape=jax.ShapeDtypeStruct((m, d), F32),
        grid=(m // tm,),
        in_specs=[row, row, vec, vec],
        out_specs=row,
        compiler_params=_cparams(("arbitrary",)),
        name="ln_add",
    )(x, mix, _r3(g), _r3(b))


def _conv_rows(x, tail, cw, cb):
    t = x.shape[0]
    ext = jnp.concatenate([tail, x], axis=0)
    y = cb + cw[CONV_W - 1:CONV_W] * x
    for j in range(CONV_W - 1):
        off = V7X_SUBLANES - (CONV_W - 1) + j
        y = y + cw[j:j + 1] * ext[off:off + t]
    return y


def _scan_rows(a, b):
    t = a.shape[0]
    row = lax.broadcasted_iota(I32, a.shape, 0)
    d = 1
    while d < t:
        keep = row >= d
        a_sh = jnp.where(keep, pltpu.roll(a, d, 0), 1.0)
        b_sh = jnp.where(keep, pltpu.roll(b, d, 0), 0.0)
        b = a * b_sh + b
        a = a * a_sh
        d *= 2
    return a, b


def _rg_gates(xc, wr, br, wi, bi, lam, passes):
    r = jax.nn.sigmoid(_mxdot(xc, wr, passes) + br)
    i = jax.nn.sigmoid(_mxdot(xc, wi, passes) + bi)
    log_a = -RG_C * jax.nn.softplus(-lam) * r
    a = jnp.exp(log_a)
    bt = jnp.sqrt(jnp.tanh(-log_a) * (a * a + 1.0)) * (i * xc)
    return a, bt


def _rglru_kernel(xa_ref, ga_ref, cw_ref, cb_ref, wr_ref, br_ref, wi_ref, bi_ref, lam_ref,
                  y_ref, hl_ref, tail_scr, h_scr, *, tc, nblk, bs, passes):
    @pl.when(pl.program_id(1) == 0)
    def _():
        tail_scr[...] = jnp.zeros_like(tail_scr)
        h_scr[...] = jnp.zeros_like(h_scr)

    for n in range(nblk):
        sl = slice(n * bs, (n + 1) * bs)
        x = xa_ref[:, sl]
        xc = _conv_rows(x, tail_scr[:, sl], cw_ref[0, :, sl], cb_ref[:, sl])
        a, bt = _rg_gates(xc, wr_ref[0, n], br_ref[:, sl], wi_ref[0, n], bi_ref[:, sl], lam_ref[:, sl],
                          passes)
        pa, hb = _scan_rows(a, bt)
        h = hb + pa * h_scr[0:1, sl]
        y_ref[:, sl] = h * jax.nn.gelu(ga_ref[:, sl])
        hl = h[tc - 1:tc, :]
        h_scr[:, sl] = jnp.broadcast_to(hl, (V7X_SUBLANES, bs))
        hl_ref[0, :, sl] = hl
        tail_scr[:, sl] = x[tc - V7X_SUBLANES:, :]


def _rglru_prompt(u, nb, t, wa, layer, conv_w, conv_b, w_r, b_r, w_i, b_i, lam, tc, passes):
    nblk, bs = w_r.shape[1], w_r.shape[2]
    nc = t // tc
    vec = pl.BlockSpec((None, 1, wa), lambda b, c: (layer, 0, 0))
    wspec = pl.BlockSpec((1, nblk, bs, bs), lambda b, c: (layer, 0, 0, 0))
    y, hl = pl.pallas_call(
        functools.partial(_rglru_kernel, tc=tc, nblk=nblk, bs=bs, passes=passes),
        out_shape=(jax.ShapeDtypeStruct((nb * t, wa), F32), jax.ShapeDtypeStruct((nb, 1, wa), F32)),
        grid=(nb, nc),
        in_specs=[pl.BlockSpec((tc, wa), lambda b, c: (b * nc + c, 0)),
                  pl.BlockSpec((tc, wa), lambda b, c: (b * nc + c, 1)),
                  pl.BlockSpec((1, CONV_W, wa), lambda b, c: (layer, 0, 0)), vec, wspec, vec, wspec,
                  vec, vec],
        out_specs=(pl.BlockSpec((tc, wa), lambda b, c: (b * nc + c, 0)),
                   pl.BlockSpec((1, 1, wa), lambda b, c: (b, 0, 0))),
        scratch_shapes=[pltpu.VMEM((V7X_SUBLANES, wa), F32), pltpu.VMEM((V7X_SUBLANES, wa), F32)],
        compiler_params=_cparams(("arbitrary", "arbitrary")),
        name="rglru_prompt",
    )(u, u, conv_w, _r3(conv_b), w_r, _r3(b_r), w_i, _r3(b_i), _r3(lam))
    return y, hl[:, 0]


def _rglru_step_kernel(xa_ref, ga_ref, buf_ref, h0_ref, cw_ref, cb_ref, wr_ref, br_ref, wi_ref,
                       bi_ref, lam_ref, y_ref, h_ref, *, nblk, bs, passes):
    for n in range(nblk):
        sl = slice(n * bs, (n + 1) * bs)
        x = xa_ref[:, sl]
        xc = cb_ref[:, sl] + cw_ref[0, CONV_W - 1:CONV_W, sl] * x
        for j in range(CONV_W - 1):
            xc = xc + cw_ref[0, j:j + 1, sl] * buf_ref[j, :, sl]
        a, bt = _rg_gates(xc, wr_ref[0, n], br_ref[:, sl], wi_ref[0, n], bi_ref[:, sl], lam_ref[:, sl],
                          passes)
        h = a * h0_ref[0, :, sl] + bt
        h_ref[:, sl] = h
        y_ref[:, sl] = h * jax.nn.gelu(ga_ref[:, sl])


def _rglru_step(u, wa, layer, buf, h0, conv_w, conv_b, w_r, b_r, w_i, b_i, lam, passes):
    nblk, bs = w_r.shape[1], w_r.shape[2]
    nb = u.shape[0]
    vec = pl.BlockSpec((None, 1, wa), lambda i: (layer, 0, 0))
    wspec = pl.BlockSpec((1, nblk, bs, bs), lambda i: (layer, 0, 0, 0))
    rows = pl.BlockSpec((nb, wa), lambda i: (0, 0))
    return pl.pallas_call(
        functools.partial(_rglru_step_kernel, nblk=nblk, bs=bs, passes=passes),
        out_shape=(jax.ShapeDtypeStruct((nb, wa), F32), jax.ShapeDtypeStruct((nb, wa), F32)),
        grid=(1,),
        in_specs=[rows, pl.BlockSpec((nb, wa), lambda i: (0, 1)),
                  pl.BlockSpec((CONV_W - 1, nb, wa), lambda i: (0, 0, 0)),
                  pl.BlockSpec((1, nb, wa), lambda i: (layer, 0, 0)),
                  pl.BlockSpec((1, CONV_W, wa), lambda i: (layer, 0, 0)), vec, wspec, vec, wspec, vec, vec],
        out_specs=(rows, rows),
        compiler_params=_cparams(("arbitrary",)),
        name="rglru_step",
    )(u, u, buf, h0, conv_w, _r3(conv_b), w_r, _r3(b_r), w_i, _r3(b_i), _r3(lam))


def _rope_kernel(q_ref, k_ref, cos_ref, sin_ref, qo_ref, ko_ref, *, nchunk, hd, qscale):
    cos = cos_ref[...]
    sin = sin_ref[...]
    for j in range(nchunk):
        sl = slice(j * hd, (j + 1) * hd)
        q = q_ref[:, sl]
        k = k_ref[:, sl]
        qo_ref[:, sl] = (q * cos + pltpu.roll(q, hd // 2, 1) * sin) * qscale
        ko_ref[:, sl] = k * cos + pltpu.roll(k, hd // 2, 1) * sin


def _rope(u, qcol, kcol, width, cos2, sin2, hd, tm):
    m = u.shape[0]
    nt = cos2.shape[0] // tm
    return pl.pallas_call(
        functools.partial(_rope_kernel, nchunk=width // hd, hd=hd, qscale=hd ** -0.5),
        out_shape=(jax.ShapeDtypeStruct((m, width), F32), jax.ShapeDtypeStruct((m, width), F32)),
        grid=(m // tm,),
        in_specs=[pl.BlockSpec((tm, width), lambda i: (i, qcol)),
                  pl.BlockSpec((tm, width), lambda i: (i, kcol)),
                  pl.BlockSpec((tm, hd), lambda i: (i % nt, 0)),
                  pl.BlockSpec((tm, hd), lambda i: (i % nt, 0))],
        out_specs=(pl.BlockSpec((tm, width), lambda i: (i, 0)),
                   pl.BlockSpec((tm, width), lambda i: (i, 0))),
        compiler_params=_cparams(("arbitrary",)),
        name="rope",
    )(u, u, cos2, sin2)


def _rope_tables(pos, hd):
    half = hd // 2
    inv = ROPE_THETA ** (-jnp.arange(half, dtype=F32) / half)
    ang = pos.astype(F32)[:, None] * inv[None, :]
    cos, sin = jnp.cos(ang), jnp.sin(ang)
    return jnp.concatenate([cos, cos], -1), jnp.concatenate([-sin, sin], -1)


def _diff_lambda(lq1, lk1, lq2, lk2, lam_init):
    return (jnp.exp(jnp.sum(lq1 * lk1, axis=-1, keepdims=True))
            - jnp.exp(jnp.sum(lq2 * lk2, axis=-1, keepdims=True)) + lam_init)


def _sub_norm(o, subg, lam_init):
    o = o * lax.rsqrt(jnp.mean(o * o, axis=-1, keepdims=True) + LN_EPS) * subg
    return o * (1.0 - lam_init)


def _attn_kernel(q_ref, k_ref, v_ref, lq1_ref, lk1_ref, lq2_ref, lk2_ref, subg_ref, o_ref,
                 m_scr, l_scr, acc_scr, *, bq, bk, hd, lam_init, passes):
    qi = pl.program_id(2)
    ki = pl.program_id(3)

    @pl.when(ki == 0)
    def _():
        m_scr[...] = jnp.full_like(m_scr, NEG_BIG)
        l_scr[...] = jnp.zeros_like(l_scr)
        acc_scr[...] = jnp.zeros_like(acc_scr)

    @pl.when(ki * bk <= qi * bq + (bq - 1))
    def _():
        q = q_ref[...]
        k = k_ref[...]
        if passes == 1:
            vh, vl = v_ref[...].astype(BF16), None
        else:
            vh, vl = _split(v_ref[...])
        rows = qi * bq + lax.broadcasted_iota(I32, (bq, bk), 0)
        cols = ki * bk + lax.broadcasted_iota(I32, (bq, bk), 1)
        mask = cols <= rows
        for c in range(2):
            s = _mxdot(q[:, c * hd:(c + 1) * hd], k[:, c * hd:(c + 1) * hd], passes, NT)
            s = jnp.where(mask, s, NEG_BIG)
            m_prev = m_scr[c]
            m_new = jnp.maximum(m_prev, jnp.max(s, axis=1, keepdims=True))
            alpha = jnp.exp(m_prev - m_new)
            p = jnp.exp(s - m_new)
            l_scr[c] = alpha * l_scr[c] + jnp.sum(p, axis=1, keepdims=True)
            acc_scr[c] = alpha * acc_scr[c] + _mxdot_pre(p, vh, vl, passes)
            m_scr[c] = m_new

    @pl.when(ki == pl.num_programs(3) - 1)
    def _():
        lam = _diff_lambda(lq1_ref[...], lk1_ref[...], lq2_ref[...], lk2_ref[...], lam_init)
        o = acc_scr[0] / l_scr[0] - lam * (acc_scr[1] / l_scr[1])
        o_ref[...] = _sub_norm(o, subg_ref[...], lam_init)


def _attn_prompt(qr, kr, u, vcol, nb, t, nh, hd, layer, lq1, lk1, lq2, lk2, subg, lam_init, bq, bk,
                 passes):
    vd = 2 * hd
    nq, nk = t // bq, t // bk

    def kv_row(b, qi, ki):
        return b * nk + jnp.minimum(ki, (qi * bq + bq - 1) // bk)

    vec = pl.BlockSpec((None, 1, hd), lambda b, h, qi, ki: (layer, 0, 0))
    return pl.pallas_call(
        functools.partial(_attn_kernel, bq=bq, bk=bk, hd=hd, lam_init=lam_init, passes=passes),
        out_shape=jax.ShapeDtypeStruct((nb * t, nh * vd), F32),
        grid=(nb, nh, nq, nk),
        in_specs=[pl.BlockSpec((bq, vd), lambda b, h, qi, ki: (b * nq + qi, h)),
                  pl.BlockSpec((bk, vd), lambda b, h, qi, ki: (kv_row(b, qi, ki), h)),
                  pl.BlockSpec((bk, vd), lambda b, h, qi, ki: (kv_row(b, qi, ki), vcol + h)),
                  vec, vec, vec, vec,
                  pl.BlockSpec((None, 1, vd), lambda b, h, qi, ki: (layer, 0, 0))],
        out_specs=pl.BlockSpec((bq, vd), lambda b, h, qi, ki: (b * nq + qi, h)),
        scratch_shapes=[pltpu.VMEM((2, bq, 1), F32), pltpu.VMEM((2, bq, 1), F32),
                        pltpu.VMEM((2, bq, vd), F32)],
        compiler_params=_cparams(("arbitrary", "arbitrary", "arbitrary", "arbitrary")),
        name="attn_prompt",
    )(qr, kr, u, _r3(lq1), _r3(lk1), _r3(lq2), _r3(lk2), _r3(subg))


def _dec_attn_kernel(pt_ref, q_ref, kn_ref, vn_ref, kc_ref, vc_ref, lq1_ref, lk1_ref, lq2_ref,
                     lk2_ref, subg_ref, o_ref, m_scr, l_scr, acc_scr, *, hd, lam_init):
    del pt_ref
    j = pl.program_id(1)
    q = q_ref[0]

    @pl.when(j == 0)
    def _():
        m_scr[...] = jnp.full_like(m_scr, NEG_BIG)
        l_scr[...] = jnp.zeros_like(l_scr)
        acc_scr[...] = jnp.zeros_like(acc_scr)

    k = kc_ref[0, 0]
    v = vc_ref[0, 0]
    for c in range(2):
        cs = slice(c * hd, (c + 1) * hd)
        s = jnp.sum(k[:, :, cs] * q[None, :, cs], axis=-1, keepdims=True)
        m_prev = m_scr[c]
        m_new = jnp.maximum(m_prev, jnp.max(s, axis=0))
        alpha = jnp.exp(m_prev - m_new)
        p = jnp.exp(s - m_new[None])
        l_scr[c] = alpha * l_scr[c] + jnp.sum(p, axis=0)
        acc_scr[c] = alpha * acc_scr[c] + jnp.sum(p * v, axis=0)
        m_scr[c] = m_new

    @pl.when(j == pl.num_programs(1) - 1)
    def _():
        kn = kn_ref[0]
        vn = vn_ref[0]
        outs = []
        for c in range(2):
            cs = slice(c * hd, (c + 1) * hd)
            s = jnp.sum(kn[:, cs] * q[:, cs], axis=-1, keepdims=True)
            m_prev = m_scr[c]
            m_new = jnp.maximum(m_prev, s)
            alpha = jnp.exp(m_prev - m_new)
            p = jnp.exp(s - m_new)
            l = alpha * l_scr[c] + p
            outs.append((alpha * acc_scr[c] + p * vn) / l)
        lam = _diff_lambda(lq1_ref[...], lk1_ref[...], lq2_ref[...], lk2_ref[...], lam_init)
        o_ref[0] = _sub_norm(outs[0] - lam * outs[1], subg_ref[...], lam_init)


def _attn_decode(page_table, layer, qs, ks, vs, cache_k, cache_v, nh, hd, lq1, lk1, lq2, lk2, subg,
                 lam_init):
    nb, n_pages = page_table.shape
    page = cache_k.shape[2]
    vd = 2 * hd
    row = pl.BlockSpec((1, nh, vd), lambda b, j, pt: (b, 0, 0))
    pg = pl.BlockSpec((1, 1, page, nh, vd), lambda b, j, pt: (layer, pt[b * n_pages + j], 0, 0, 0))
    vec = pl.BlockSpec((None, 1, hd), lambda b, j, pt: (layer, 0, 0))
    grid_spec = pltpu.PrefetchScalarGridSpec(
        num_scalar_prefetch=1,
        grid=(nb, n_pages),
        in_specs=[row, row, row, pg, pg, vec, vec, vec, vec,
                  pl.BlockSpec((None, 1, vd), lambda b, j, pt: (layer, 0, 0))],
        out_specs=row,
        scratch_shapes=[pltpu.VMEM((2, nh, 1), F32), pltpu.VMEM((2, nh, 1), F32),
                        pltpu.VMEM((2, nh, vd), F32)],
    )
    out = pl.pallas_call(
        functools.partial(_dec_attn_kernel, hd=hd, lam_init=lam_init),
        out_shape=jax.ShapeDtypeStruct((nb, nh, vd), F32),
        grid_spec=grid_spec,
        compiler_params=_cparams(("arbitrary", "arbitrary")),
        name="attn_decode",
    )(page_table.reshape(-1), qs.reshape(nb, nh, vd), ks.reshape(nb, nh, vd), vs.reshape(nb, nh, vd),
      cache_k, cache_v, _r3(lq1), _r3(lk1), _r3(lq2), _r3(lk2), _r3(subg))
    return out.reshape(nb, nh * vd)


def _expand_block_diag(w, lanes):
    n, bs, _ = w.shape
    g = lanes // bs
    w4 = w.reshape(n // g, g, bs, bs)
    eye = jnp.eye(g, dtype=w.dtype)
    dense = w4[:, :, :, None, :] * eye[None, :, None, :, None]
    return dense.reshape(n // g, lanes, lanes)


def _ml_qkv(xc, xm, wqk_ref, wv_ref, q_ref, k_ref, v_ref, kscale, passes):
    lanes = wv_ref.shape[1]
    for j in range(wv_ref.shape[0]):
        sl = slice(j * lanes, (j + 1) * lanes)
        qk = _mxdot(xc[:, sl], wqk_ref[j], passes)
        q_ref[:, sl] = qk[:, :lanes]
        k_ref[:, sl] = qk[:, lanes:] * kscale
        v_ref[:, sl] = _mxdot(xm[:, sl], wv_ref[j], passes)


def _ml_pre_kernel(xm_ref, cw_ref, cb_ref, wqk_ref, wv_ref, xc_ref, q_ref, k_ref, v_ref, tail_scr,
                   *, tc, kscale, passes):
    @pl.when(pl.program_id(1) == 0)
    def _():
        tail_scr[...] = jnp.zeros_like(tail_scr)

    x = xm_ref[...]
    xc = jax.nn.silu(_conv_rows(x, tail_scr[...], cw_ref[0], cb_ref[...]))
    tail_scr[...] = x[tc - V7X_SUBLANES:, :]
    xc_ref[...] = xc
    _ml_qkv(xc, x, wqk_ref, wv_ref, q_ref, k_ref, v_ref, kscale, passes)


def _ml_pre_prompt(u, nb, t, wc, layer, conv_w, conv_b, wqk, wv, kscale, tc, passes):
    nc = t // tc
    nt, lanes = wv.shape[0], wv.shape[1]
    rows = pl.BlockSpec((tc, wc), lambda b, c: (b * nc + c, 0))
    return pl.pallas_call(
        functools.partial(_ml_pre_kernel, tc=tc, kscale=kscale, passes=passes),
        out_shape=(jax.ShapeDtypeStruct((nb * t, wc), F32),) * 4,
        grid=(nb, nc),
        in_specs=[rows, pl.BlockSpec((1, CONV_W, wc), lambda b, c: (layer, 0, 0)),
                  pl.BlockSpec((None, 1, wc), lambda b, c: (layer, 0, 0)),
                  pl.BlockSpec((nt, lanes, 2 * lanes), lambda b, c: (0, 0, 0)),
                  pl.BlockSpec((nt, lanes, lanes), lambda b, c: (0, 0, 0))],
        out_specs=(rows, rows, rows, rows),
        scratch_shapes=[pltpu.VMEM((V7X_SUBLANES, wc), F32)],
        compiler_params=_cparams(("arbitrary", "arbitrary")),
        name="mlstm_pre_prompt",
    )(u, conv_w, _r3(conv_b), wqk, wv)


def _ml_pre_step_kernel(xm_ref, buf_ref, cw_ref, cb_ref, wqk_ref, wv_ref, xc_ref, q_ref, k_ref,
                        v_ref, *, kscale, passes):
    x = xm_ref[...]
    xc = cb_ref[...] + cw_ref[0, CONV_W - 1:CONV_W, :] * x
    for j in range(CONV_W - 1):
        xc = xc + cw_ref[0, j:j + 1, :] * buf_ref[j]
    xc = jax.nn.silu(xc)
    xc_ref[...] = xc
    _ml_qkv(xc, x, wqk_ref, wv_ref, q_ref, k_ref, v_ref, kscale, passes)


def _ml_pre_step(u, wc, layer, buf, conv_w, conv_b, wqk, wv, kscale, passes):
    nb = u.shape[0]
    nt, lanes = wv.shape[0], wv.shape[1]
    rows = pl.BlockSpec((nb, wc), lambda i: (0, 0))
    return pl.pallas_call(
        functools.partial(_ml_pre_step_kernel, kscale=kscale, passes=passes),
        out_shape=(jax.ShapeDtypeStruct((nb, wc), F32),) * 4,
        grid=(1,),
        in_specs=[rows, pl.BlockSpec((CONV_W - 1, nb, wc), lambda i: (0, 0, 0)),
                  pl.BlockSpec((1, CONV_W, wc), lambda i: (layer, 0, 0)),
                  pl.BlockSpec((None, 1, wc), lambda i: (layer, 0, 0)),
                  pl.BlockSpec((nt, lanes, 2 * lanes), lambda i: (0, 0, 0)),
                  pl.BlockSpec((nt, lanes, lanes), lambda i: (0, 0, 0))],
        out_specs=(rows, rows, rows, rows),
        compiler_params=_cparams(("arbitrary",)),
        name="mlstm_pre_step",
    )(u, buf, conv_w, _r3(conv_b), wqk, wv)


def _ml_out(h, z, xc, gng, skip):
    mu = jnp.mean(h, axis=-1, keepdims=True)
    hc = h - mu
    var = jnp.mean(hc * hc, axis=-1, keepdims=True)
    hn = hc * lax.rsqrt(var + LN_EPS) * gng
    return jax.nn.sigmoid(z) * (hn + skip * xc)


def _ml_scan_kernel(q_ref, k_ref, v_ref, z_ref, xc_ref, grow_ref, gcol_ref, brow_ref, bcol_ref,
                    gng_ref, skip_ref, pre_ref, c_ref, n_ref, m_ref, c_scr, n_scr, m_scr,
                    *, L, nh, passes):
    hh = pl.program_id(1)
    ci = pl.program_id(2)

    @pl.when(ci == 0)
    def _():
        c_scr[...] = jnp.zeros_like(c_scr)
        n_scr[...] = jnp.zeros_like(n_scr)
        m_scr[...] = jnp.zeros_like(m_scr)

    g_rows = grow_ref[0] + bcol_ref[...]
    sub = lax.broadcasted_iota(I32, g_rows.shape, 0)
    li_row = jnp.sum(jnp.where(sub == hh, g_rows, 0.0), axis=0, keepdims=True)
    lf_row = jax.nn.log_sigmoid(jnp.sum(jnp.where(sub == hh + nh, g_rows, 0.0), axis=0, keepdims=True))
    g_cols = gcol_ref[...] + brow_ref[...]
    lane = lax.broadcasted_iota(I32, g_cols.shape, 1)
    li_col = jnp.sum(jnp.where(lane == hh, g_cols, 0.0), axis=1, keepdims=True)
    lf_col = jax.nn.log_sigmoid(jnp.sum(jnp.where(lane == hh + nh, g_cols, 0.0), axis=1, keepdims=True))

    tt = lax.broadcasted_iota(I32, (L, L), 0)
    ss = lax.broadcasted_iota(I32, (L, L), 1)
    causal = ss <= tt
    f_col = jnp.sum(jnp.where(causal, jnp.broadcast_to(lf_row, (L, L)), 0.0), axis=1, keepdims=True)
    f_row = jnp.sum(jnp.where(tt <= ss, jnp.broadcast_to(lf_col, (L, L)), 0.0), axis=0, keepdims=True)
    f_last = jnp.sum(lf_col, axis=0, keepdims=True)

    m0 = m_scr[0:1, 0:1]
    dm = jnp.where(causal, f_col - f_row + li_row, -jnp.inf)
    bt = f_col + m0
    m = jnp.maximum(bt, jnp.max(dm, axis=1, keepdims=True))
    wd = jnp.exp(dm - m)
    inter = jnp.exp(bt - m)

    q = q_ref[...]
    k = k_ref[...]
    v = v_ref[...]
    c0 = c_scr[...]
    n0 = n_scr[0:1, :]
    if passes == 1:
        vh, vl = v.astype(BF16), None
    else:
        vh, vl = _split(v)
    s = _mxdot(q, k, passes, NT) * wd
    num = _mxdot_pre(s, vh, vl, passes) + inter * _mxdot(q, c0, passes)
    den = jnp.sum(s, axis=1, keepdims=True) + inter * jnp.sum(q * n0, axis=1, keepdims=True)
    h = num / jnp.maximum(jnp.abs(den), jnp.exp(-m))

    m_last = m[L - 1:L, :]
    wk_col = jnp.exp(f_last - f_col + li_col - m_last)
    decay = jnp.exp(f_last + m0 - m_last)
    kw = k * wk_col
    c_new = decay * c0 + _mxdot_pre(kw, vh, vl, passes, TN)
    n_new = decay * n0 + jnp.sum(kw, axis=0, keepdims=True)
    c_scr[...] = c_new
    n_scr[...] = jnp.broadcast_to(n_new, n_scr.shape)
    m_scr[...] = jnp.broadcast_to(m_last, m_scr.shape)

    pre_ref[...] = _ml_out(h, z_ref[...], xc_ref[...], gng_ref[...], skip_ref[...])

    @pl.when(ci == pl.num_programs(2) - 1)
    def _():
        c_ref[0, 0] = c_new
        n_ref[0, 0] = n_new
        m_ref[0, 0] = jnp.broadcast_to(m_last, (1, V7X_LANES))


def _ml_scan_prompt(q, k, v, u, zcol, xc, g_rows, g_cols, b_rows, b_cols, layer, gng, skip, nb, t, nh,
                    dh, L, passes):
    nc = t // L
    wc = nh * dh
    blk = lambda col0: pl.BlockSpec((L, dh), lambda b, h, c: (b * nc + c, col0 + h))
    vec = pl.BlockSpec((None, 1, dh), lambda b, h, c: (layer, 0, h))
    return pl.pallas_call(
        functools.partial(_ml_scan_kernel, L=L, nh=nh, passes=passes),
        out_shape=(jax.ShapeDtypeStruct((nb * t, wc), F32),
                   jax.ShapeDtypeStruct((nb, nh, dh, dh), F32),
                   jax.ShapeDtypeStruct((nb, nh, 1, dh), F32),
                   jax.ShapeDtypeStruct((nb, nh, 1, V7X_LANES), F32)),
        grid=(nb, nh, nc),
        in_specs=[blk(0), blk(0), blk(0), blk(zcol), blk(0),
                  pl.BlockSpec((1, 2 * nh, L), lambda b, h, c: (b, 0, c)),
                  pl.BlockSpec((L, 2 * nh), lambda b, h, c: (b * nc + c, 0)),
                  pl.BlockSpec((1, 2 * nh), lambda b, h, c: (0, 0)),
                  pl.BlockSpec((2 * nh, 1), lambda b, h, c: (0, 0)),
                  vec, vec],
        out_specs=(blk(0),
                   pl.BlockSpec((1, 1, dh, dh), lambda b, h, c: (b, h, 0, 0)),
                   pl.BlockSpec((1, 1, 1, dh), lambda b, h, c: (b, h, 0, 0)),
                   pl.BlockSpec((1, 1, 1, V7X_LANES), lambda b, h, c: (b, h, 0, 0))),
        scratch_shapes=[pltpu.VMEM((dh, dh), F32), pltpu.VMEM((V7X_SUBLANES, dh), F32),
                        pltpu.VMEM((V7X_SUBLANES, V7X_LANES), F32)],
        compiler_params=_cparams(("arbitrary", "arbitrary", "arbitrary")),
        name="mlstm_scan_prompt",
    )(q, k, v, u, xc, g_rows, g_cols, b_rows, b_cols, _r3(gng), _r3(skip))


def _ml_step_kernel(q_ref, k_ref, v_ref, z_ref, xc_ref, li_ref, fg_ref, m0_ref, c0_ref, n0_ref,
                    gng_ref, skip_ref, pre_ref, c_ref, n_ref, m_ref, *, passes):
    q = q_ref[0]
    k = k_ref[0]
    v = v_ref[0]
    c0 = c0_ref[0, 0, 0]
    n0 = n0_ref[0, 0, 0]
    li = li_ref[0, 0][:, 0:1]
    lf = jax.nn.log_sigmoid(fg_ref[0, 0][:, 0:1])
    m0 = m0_ref[0, 0][:, 0:1]
    bt = lf + m0
    m = jnp.maximum(bt, li)
    wd = jnp.exp(li - m)
    inter = jnp.exp(bt - m)
    rows = V7X_SUBLANES
    dh = q.shape[1]
    first = lax.broadcasted_iota(I32, (rows, dh), 0) == 0
    q8 = jnp.where(first, jnp.broadcast_to(q, (rows, dh)), 0.0)
    k8 = jnp.where(first, jnp.broadcast_to(k, (rows, dh)), 0.0)
    v8 = jnp.where(first, jnp.broadcast_to(v, (rows, dh)), 0.0)
    s = jnp.sum(q * k, axis=1, keepdims=True) * wd
    qc = _mxdot(q8, c0, passes)[0:1, :]
    num = s * v + inter * qc
    den = s + inter * jnp.sum(q * n0, axis=1, keepdims=True)
    h = num / jnp.maximum(jnp.abs(den), jnp.exp(-m))
    outer = _mxdot(k8 * wd, v8, passes, TN)
    c_ref[0, 0] = inter * c0 + outer
    n_ref[0, 0] = inter * n0 + wd * k
    m_ref[0, 0] = jnp.broadcast_to(m, (1, V7X_LANES))
    pre_ref[0] = _ml_out(h, z_ref[0], xc_ref[0], gng_ref[...], skip_ref[...])


def _ml_step(q, k, v, u, zcol, xc, li, fg, layer, m0, c0, n0, gng, skip, nh, dh, passes):
    nb = q.shape[0]
    wc = nh * dh
    r3 = lambda a: a.reshape(nb, 1, a.shape[1])
    blk = lambda col0: pl.BlockSpec((1, 1, dh), lambda b, h: (b, 0, col0 + h))
    sc = pl.BlockSpec((1, 1, 1, V7X_LANES), lambda b, h: (b, h, 0, 0))
    vec = pl.BlockSpec((None, 1, dh), lambda b, h: (layer, 0, h))
    rep = lambda a: jnp.broadcast_to(a[:, :, None, None], (nb, nh, 1, V7X_LANES))
    n0r = n0.reshape(n0.shape[0], nb, nh, 1, dh)
    pre, c, n, m = pl.pallas_call(
        functools.partial(_ml_step_kernel, passes=passes),
        out_shape=(jax.ShapeDtypeStruct((nb, 1, wc), F32),
                   jax.ShapeDtypeStruct((nb, nh, dh, dh), F32),
                   jax.ShapeDtypeStruct((nb, nh, 1, dh), F32),
                   jax.ShapeDtypeStruct((nb, nh, 1, V7X_LANES), F32)),
        grid=(nb, nh),
        in_specs=[blk(0), blk(0), blk(0), blk(zcol), blk(0), sc, sc, sc,
                  pl.BlockSpec((1, 1, 1, dh, dh), lambda b, h: (layer, b, h, 0, 0)),
                  pl.BlockSpec((1, 1, 1, 1, dh), lambda b, h: (layer, b, h, 0, 0)), vec, vec],
        out_specs=(blk(0),
                   pl.BlockSpec((1, 1, dh, dh), lambda b, h: (b, h, 0, 0)),
                   pl.BlockSpec((1, 1, 1, dh), lambda b, h: (b, h, 0, 0)), sc),
        compiler_params=_cparams(("arbitrary", "arbitrary")),
        name="mlstm_step",
    )(r3(q), r3(k), r3(v), r3(u), r3(xc), rep(li), rep(fg), rep(m0), c0, n0r, _r3(gng), _r3(skip))
    return pre.reshape(nb, wc), c, n[:, :, 0], m[:, :, 0, 0]


def _router_kernel(x_ref, wt_ref, b_ref, idx_ref, gate_ref, pos_ref, cnt_ref, *, n_exp, n_grp,
                   n_valid, passes):
    per = n_exp // n_grp
    tm = x_ref.shape[0]
    logits = _mxdot(wt_ref[...], x_ref[...], passes, NT)
    s_all = jax.nn.sigmoid(logits)
    sb_all = s_all + b_ref[...]
    s = [s_all[e:e + 1, :] for e in range(n_exp)]
    sb = [sb_all[e:e + 1, :] for e in range(n_exp)]

    def top2_sum(a, b, c, d):
        hi1, lo1 = jnp.maximum(a, b), jnp.minimum(a, b)
        hi2, lo2 = jnp.maximum(c, d), jnp.minimum(c, d)
        return jnp.maximum(hi1, hi2) + jnp.maximum(jnp.minimum(hi1, hi2), jnp.maximum(lo1, lo2))

    assert per == 4
    gscore = [top2_sum(*sb[g * per:(g + 1) * per]) for g in range(n_grp)]
    best, gi = gscore[0], jnp.zeros_like(gscore[0], dtype=I32)
    for g in range(1, n_grp):
        take = gscore[g] > best
        gi = jnp.where(take, g, gi)
        best = jnp.maximum(best, gscore[g])

    def pick(vals, j):
        out = vals[j]
        for g in range(1, n_grp):
            out = jnp.where(gi == g, vals[g * per + j], out)
        return out

    v = [pick(sb, j) for j in range(per)]
    w = [pick(s, j) for j in range(per)]
    rank = []
    for j in range(per):
        r = jnp.zeros_like(gi)
        for i in range(per):
            if i == j:
                continue
            ahead = (v[i] >= v[j]) if i < j else (v[i] > v[j])
            r = r + ahead.astype(I32)
        rank.append(r)
    loc0 = sum(jnp.where(rank[j] == 0, j, 0) for j in range(per))
    loc1 = sum(jnp.where(rank[j] == 1, j, 0) for j in range(per))
    w0 = sum(jnp.where(rank[j] == 0, w[j], 0.0) for j in range(per))
    w1 = sum(jnp.where(rank[j] == 1, w[j], 0.0) for j in range(per))
    tot = w0 + w1
    valid = lax.broadcasted_iota(I32, (1, tm), 1) < n_valid
    e0 = jnp.where(valid, gi * per + loc0, -1)
    e1 = jnp.where(valid, gi * per + loc1, -1)
    idx_ref[0:1, :] = e0
    idx_ref[1:2, :] = e1
    gate_ref[0:1, :] = w0 / tot
    gate_ref[1:2, :] = w1 / tot
    eio = lax.broadcasted_iota(I32, (n_exp, tm), 0)
    oh0 = eio == e0
    oh1 = eio == e1
    sel = jnp.where(oh0, 1.0, jnp.where(oh1, 1.0, 0.0))
    earlier = (lax.broadcasted_iota(I32, (tm, tm), 0) < lax.broadcasted_iota(I32, (tm, tm), 1))
    prefix = _dg(sel.astype(BF16), jnp.where(earlier, 1.0, 0.0).astype(BF16), NN)
    pos_ref[0:1, :] = jnp.sum(jnp.where(oh0, prefix, 0.0), axis=0, keepdims=True).astype(I32)
    pos_ref[1:2, :] = jnp.sum(jnp.where(oh1, prefix, 0.0), axis=0, keepdims=True).astype(I32)
    cnt_ref[0] = jnp.broadcast_to(jnp.sum(sel, axis=1, keepdims=True), (n_exp, V7X_LANES)).astype(I32)


def _router(x, w_router_t, b_router, tm, n_valid, passes):
    m, d = x.shape
    n_exp = w_router_t.shape[0]
    col = pl.BlockSpec((TOP_K, tm), lambda i: (0, i))
    idx, gate, pos, cnt = pl.pallas_call(
        functools.partial(_router_kernel, n_exp=n_exp, n_grp=N_GROUPS, n_valid=n_valid, passes=passes),
        out_shape=(jax.ShapeDtypeStruct((TOP_K, m), I32), jax.ShapeDtypeStruct((TOP_K, m), F32),
                   jax.ShapeDtypeStruct((TOP_K, m), I32),
                   jax.ShapeDtypeStruct((m // tm, n_exp, V7X_LANES), I32)),
        grid=(m // tm,),
        in_specs=[pl.BlockSpec((tm, d), lambda i: (i, 0)),
                  pl.BlockSpec((n_exp, d), lambda i: (0, 0)),
                  pl.BlockSpec((n_exp, 1), lambda i: (0, 0))],
        out_specs=(col, col, col, pl.BlockSpec((1, n_exp, V7X_LANES), lambda i: (i, 0, 0))),
        compiler_params=_cparams(("arbitrary",)),
        name="router",
    )(x, w_router_t, b_router.reshape(n_exp, 1))
    return idx, gate, pos, cnt[:, :, 0]


def _dispatch_plan(idx, pos, cnt, tile_rows, n_exp, bm, n_blocks):
    tile_off = jnp.cumsum(cnt, axis=0) - cnt
    counts = jnp.sum(cnt, axis=0)
    padded = (counts + bm - 1) // bm * bm
    pad_end = jnp.cumsum(padded)
    pad_start = pad_end - padded
    base = pad_start[None, :] + tile_off
    base_tok = jnp.concatenate([jnp.broadcast_to(base[i:i + 1], (r, n_exp))
                                for i, r in enumerate(tile_rows)], axis=0)
    eio = jnp.arange(n_exp, dtype=I32)
    dest = pos + jnp.sum(jnp.where(idx[:, :, None] == eio, base_tok[None], 0), axis=-1)
    blk_start = jnp.arange(n_blocks, dtype=I32) * bm
    blk_e = jnp.minimum(jnp.sum((blk_start[:, None] >= pad_end[None, :]).astype(I32), axis=1), n_exp - 1)
    blk_n = jnp.clip(counts[blk_e] - (blk_start - pad_start[blk_e]), 0, bm)
    used = blk_start < pad_end[-1]
    blk_n = jnp.where(used, blk_n, 0).astype(I32)
    last_e = jnp.max(jnp.where(used, blk_e, 0))
    blk_e = jnp.where(used, blk_e, last_e).astype(I32)
    return dest.astype(I32).reshape(-1), blk_e, blk_n


def _dispatch_kernel(dest, xp_hbm, xs_hbm, xb_hbm, sem, *, n_prompt, n_sample, batch):
    nt = n_prompt + n_sample

    def row_copy(src_ref, row, a):
        return pltpu.make_async_copy(src_ref.at[pl.ds(row, 1)], xb_hbm.at[pl.ds(dest[a], 1)], sem)

    def drain(n):
        def body(r, c):
            row_copy(xp_hbm, 0, 0).wait()
            return c
        lax.fori_loop(0, n, body, 0)

    for r in range(n_sample):
        for kk in range(TOP_K):
            row_copy(xs_hbm, r, kk * nt + n_prompt + r).start()

    def batch_body(bi, c):
        def body(r, c2):
            t = bi * batch + r
            for kk in range(TOP_K):
                row_copy(xp_hbm, t, kk * nt + t).start()
            return c2
        lax.fori_loop(0, batch, body, 0, unroll=8)

        @pl.when(bi > 0)
        def _():
            drain(TOP_K * batch)
        return c

    lax.fori_loop(0, n_prompt // batch, batch_body, 0)
    drain(TOP_K * batch + TOP_K * n_sample)


def _dispatch(dest, x_p, x_s, rows, batch):
    n_prompt, d = x_p.shape
    any_spec = pl.BlockSpec(memory_space=pl.ANY)
    grid_spec = pltpu.PrefetchScalarGridSpec(
        num_scalar_prefetch=1, grid=(1,), in_specs=[any_spec, any_spec], out_specs=any_spec,
        scratch_shapes=[pltpu.SemaphoreType.DMA])
    return pl.pallas_call(
        functools.partial(_dispatch_kernel, n_prompt=n_prompt, n_sample=x_s.shape[0], batch=batch),
        out_shape=jax.ShapeDtypeStruct((rows, d), F32),
        grid_spec=grid_spec,
        compiler_params=_cparams(("arbitrary",)),
        name="moe_dispatch",
    )(dest, x_p, x_s)


def _experts_kernel(blk_e, blk_n, *refs, bm, passes, has_prev):
    if has_prev:
        xb_ref, yprev_ref, w1_ref, w3_ref, w2_ref, y_ref, w1h, w1l, w3h, w3l, w2h, w2l = refs
    else:
        xb_ref, w1_ref, w3_ref, w2_ref, y_ref, w1h, w1l, w3h, w3l, w2h, w2l = refs
        yprev_ref = None
    i = pl.program_id(0)
    n = blk_n[i]

    @pl.when(n == 0)
    def _():
        y_ref[...] = jnp.zeros_like(y_ref)

    @pl.when(n > 0)
    def _():
        changed = jnp.logical_or(i == 0, blk_e[i] != blk_e[jnp.maximum(i - 1, 0)])

        @pl.when(changed)
        def _():
            _store_split(w1_ref[0, 0], w1h, w1l, passes)
            _store_split(w3_ref[0, 0], w3h, w3l, passes)
            _store_split(w2_ref[0, 0], w2h, w2l, passes)

        rows = lax.broadcasted_iota(I32, (bm, 1), 0)
        x = jnp.where(rows < n, xb_ref[...], 0.0)
        lo = (lambda r: r[...]) if passes > 1 else (lambda r: None)
        h1 = _mxdot_pre(x, w1h[...], lo(w1l), passes)
        h3 = _mxdot_pre(x, w3h[...], lo(w3l), passes)
        y = _mxdot_pre(jax.nn.silu(h1) * h3, w2h[...], lo(w2l), passes)
        y_ref[...] = y if yprev_ref is None else yprev_ref[...] + y


def _experts(xb, blk_e, blk_n, layer, w1, w3, w2, bm, n_split, passes):
    rows, d = xb.shape
    n_blocks = rows // bm
    f = w1.shape[3] // n_split
    y = None
    for j in range(n_split):
        row = pl.BlockSpec((bm, d), lambda i, be, bn: (i, 0))
        grid_spec = pltpu.PrefetchScalarGridSpec(
            num_scalar_prefetch=2,
            grid=(n_blocks,),
            in_specs=[row] * (1 if y is None else 2) + [
                pl.BlockSpec((1, 1, d, f), lambda i, be, bn, j=j: (layer, be[i], 0, j)),
                pl.BlockSpec((1, 1, d, f), lambda i, be, bn, j=j: (layer, be[i], 0, j)),
                pl.BlockSpec((1, 1, f, d), lambda i, be, bn, j=j: (layer, be[i], j, 0))],
            out_specs=row,
            scratch_shapes=[pltpu.VMEM((d, f), BF16), pltpu.VMEM(_lo_shape((d, f), passes), BF16),
                            pltpu.VMEM((d, f), BF16), pltpu.VMEM(_lo_shape((d, f), passes), BF16),
                            pltpu.VMEM((f, d), BF16), pltpu.VMEM(_lo_shape((f, d), passes), BF16)],
        )
        args = (xb,) if y is None else (xb, y)
        y = pl.pallas_call(
            functools.partial(_experts_kernel, bm=bm, passes=passes, has_prev=y is not None),
            out_shape=jax.ShapeDtypeStruct((rows, d), F32),
            grid_spec=grid_spec,
            compiler_params=_cparams(("arbitrary",)),
            name="moe_experts",
        )(blk_e, blk_n, *args, w1, w3, w2)
    return y


def _combine_ln_kernel(dest, x_ref, gate_ref, g_ref, b_ref, yb_hbm, o_ref, buf, sem, *, alpha, tm, nt,
                       tok0):
    i = pl.program_id(0)
    slot = i % 2

    def issue(tile, sl):
        def body(r, c):
            t = tok0 + tile * tm + r
            for kk in range(TOP_K):
                pltpu.make_async_copy(yb_hbm.at[pl.ds(dest[kk * nt + t], 1)],
                                      buf.at[sl, kk, pl.ds(r, 1)], sem.at[sl]).start()
            return c
        lax.fori_loop(0, tm, body, 0, unroll=8)

    @pl.when(i == 0)
    def _():
        issue(0, 0)

    @pl.when(i + 1 < pl.num_programs(0))
    def _():
        issue(i + 1, 1 - slot)

    def wait_body(r, c):
        pltpu.make_async_copy(yb_hbm.at[pl.ds(0, 1)], buf.at[slot, 0, pl.ds(0, 1)], sem.at[slot]).wait()
        return c
    lax.fori_loop(0, TOP_K * tm, wait_body, 0)

    y = alpha * x_ref[...] + gate_ref[:, 0:1] * buf[slot, 0] + gate_ref[:, 1:2] * buf[slot, 1]
    o_ref[...] = _layer_norm_rows(y, g_ref[...], b_ref[...])


def _combine_ln(dest, x, gates, yb, tok0, g, b, layer, alpha, tm):
    m, d = x.shape
    nt = gates.shape[0]
    tb0 = tok0 // tm
    assert tok0 % tm == 0 and m % tm == 0
    grid_spec = pltpu.PrefetchScalarGridSpec(
        num_scalar_prefetch=1,
        grid=(m // tm,),
        in_specs=[pl.BlockSpec((tm, d), lambda i, dst: (i, 0)),
                  pl.BlockSpec((tm, TOP_K), lambda i, dst: (i + tb0, 0)),
                  pl.BlockSpec((None, 1, d), lambda i, dst: (layer, 0, 0)),
                  pl.BlockSpec((None, 1, d), lambda i, dst: (layer, 0, 0)),
                  pl.BlockSpec(memory_space=pl.ANY)],
        out_specs=pl.BlockSpec((tm, d), lambda i, dst: (i, 0)),
        scratch_shapes=[pltpu.VMEM((2, TOP_K, tm, d), F32), pltpu.SemaphoreType.DMA((2,))],
    )
    return pl.pallas_call(
        functools.partial(_combine_ln_kernel, alpha=alpha, tm=tm, nt=nt, tok0=tok0),
        out_shape=jax.ShapeDtypeStruct((m, d), F32),
        grid_spec=grid_spec,
        compiler_params=_cparams(("arbitrary",)),
        name="moe_combine_ln",
    )(dest, x, gates, _r3(g), _r3(b), yb)


def _tile(n, want):
    t = min(n, want)
    while n % t:
        t //= 2
    return t


MOE_BLOCK_ROWS = 256
MOE_HIDDEN_SPLIT = 2
ROUTER_SAMPLE_TILE = 128


def kernel(x_prompt, x_sample, cache_k, cache_v, state_rglru_h, state_rglru_conv, state_mlstm_c, state_mlstm_n, state_mlstm_m, state_mlstm_conv, page_table, ln1_g, ln1_b, ln2_g, ln2_b, w_in_even, w_out_even, rg_conv_w, rg_conv_b, rg_w_r, rg_b_r, rg_w_i, rg_b_i, rg_lambda, da_lam_q1, da_lam_k1, da_lam_q2, da_lam_k2, da_subln_g, w_in_odd, w_out_odd, ml_conv_w, ml_conv_b, ml_w_q, ml_w_k, ml_w_v, ml_b_i, ml_b_f, ml_gn_g, ml_skip, w_router, b_router, w_e1, w_e3, w_e2):
    nb, t, d = x_prompt.shape
    nbs = x_sample.shape[0]
    assert x_sample.shape[1] == 1
    depth = ln1_g.shape[0]
    wa = rg_conv_w.shape[2]
    hd = da_lam_q1.shape[1]
    nh = cache_k.shape[3]
    qw = nh * 2 * hd
    wc = ml_conv_w.shape[2]
    mh = ml_b_i.shape[1]
    dh = wc // mh
    n_exp = w_router.shape[1]
    past_len = page_table.shape[1] * cache_k.shape[2]
    alpha = (2.0 * depth) ** 0.25
    assert wa == qw, "column blocks of the even in-projection are addressed in units of wa"

    n_p = nb * t
    nt = n_p + nbs
    xp = x_prompt.reshape(n_p, d)
    xs = x_sample.reshape(nbs, d)

    tm_p = _tile(n_p, 512)
    tn = _tile(d, 1024)
    tc = _tile(t, 256)
    bq = _tile(t, 512)
    ml_chunk = _tile(t, 256)
    tm_ln = _tile(n_p, 256)
    bm = MOE_BLOCK_ROWS
    n_blocks = -(-(TOP_K * nt + n_exp * (bm - 1)) // bm)
    tile_rows = (tm_p,) * (n_p // tm_p) + (nbs,)

    cos_p, sin_p = _rope_tables(jnp.arange(t, dtype=I32), hd)
    cos_s, sin_s = _rope_tables(jnp.full((nbs,), past_len, I32), hd)
    w_router_t = w_router.T
    kscale = dh ** -0.5
    w_gate = w_in_odd[:, :, 2 * wc:]
    P = PRECISE_PASSES

    outs = {k: [] for k in ("kp", "vp", "ks", "vs", "hp", "hs", "rcp", "rcs",
                            "cp", "cs", "np", "ns", "mp", "ms", "mcp", "mcs")}

    for l in range(depth):
        pp = P if l < depth - 1 else LAST_LAYER_PASSES
        if l % 2 == 0:
            e = l // 2
            lam_init = 0.8 - 0.6 * math.exp(-0.3 * l)
            rg = (rg_conv_w, rg_conv_b, rg_w_r, rg_b_r, rg_w_i, rg_b_i, rg_lambda)
            lams = (da_lam_q1, da_lam_k1, da_lam_q2, da_lam_k2, da_subln_g)
            n_in = w_in_even.shape[2]
            u_p = _mm([xp], w_in_even, e, 0, n_in, tm_p, tn, pp)
            ya_p, h_p = _rglru_prompt(u_p, nb, t, wa, e, *rg, tc, pp)
            qr_p, kr_p = _rope(u_p, 2, 3, qw, cos_p, sin_p, hd, tm_p)
            o_p = _attn_prompt(qr_p, kr_p, u_p, (2 * wa + 2 * qw) // (2 * hd), nb, t, nh, hd, e, *lams,
                               lam_init, bq, bq, pp)
            mix_p = _mm([ya_p, o_p], w_out_even, e, 0, d, tm_p, tn, pp)
            v_p = u_p[:, 2 * wa + 2 * qw:]
            outs["kp"].append(kr_p.reshape(nb, t, nh, 2 * hd))
            outs["vp"].append(v_p.reshape(nb, t, nh, 2 * hd))
            outs["hp"].append(h_p)
            outs["rcp"].append(u_p[:, :wa].reshape(nb, t, wa)[:, t - (CONV_W - 1):])
            u_s = _mm([xs], w_in_even, e, 0, n_in, nbs, tn, P)
            buf = state_rglru_conv[e]
            ya_s, h_s = _rglru_step(u_s, wa, e, jnp.swapaxes(buf, 0, 1), state_rglru_h, *rg, P)
            qr_s, kr_s = _rope(u_s, 2, 3, qw, cos_s, sin_s, hd, nbs)
            v_s = u_s[:, 2 * wa + 2 * qw:]
            o_s = _attn_decode(page_table, e, qr_s, kr_s, v_s, cache_k, cache_v, nh, hd, *lams, lam_init)
            mix_s = _mm([ya_s, o_s], w_out_even, e, 0, d, nbs, tn, P)
            outs["ks"].append(kr_s.reshape(nbs, 1, nh, 2 * hd))
            outs["vs"].append(v_s.reshape(nbs, 1, nh, 2 * hd))
            outs["hs"].append(h_s)
            outs["rcs"].append(jnp.concatenate([buf[:, 1:], u_s[:, None, :wa]], axis=1))
        else:
            o = l // 2
            wqk = jnp.concatenate([_expand_block_diag(ml_w_q[o], V7X_LANES),
                                   _expand_block_diag(ml_w_k[o], V7X_LANES)], axis=2)
            wv = _expand_block_diag(ml_w_v[o], V7X_LANES)
            bias = jnp.concatenate([ml_b_i[o], ml_b_f[o]])
            u_p = _mm([xp], w_in_odd, o, 0, 2 * wc, tm_p, tn, pp)
            g_p = _mm([xp], w_gate, o, 0, 2 * mh, tm_p, 2 * mh, P)
            xc_p, q_p, k_p, v_p = _ml_pre_prompt(u_p, nb, t, wc, o, ml_conv_w, ml_conv_b, wqk, wv,
                                                 kscale, tc, pp)
            g_rows = jnp.swapaxes(g_p.reshape(nb, t, 2 * mh), 1, 2)
            pre_p, c_p, nn_p, m_p = _ml_scan_prompt(
                q_p, k_p, v_p, u_p, wc // dh, xc_p, g_rows, g_p, bias.reshape(1, -1), bias.reshape(-1, 1),
                o, ml_gn_g, ml_skip, nb, t, mh, dh, ml_chunk, pp)
            mix_p = _mm([pre_p], w_out_odd, o, 0, d, tm_p, tn, pp)
            outs["cp"].append(c_p)
            outs["np"].append(nn_p[:, :, 0])
            outs["mp"].append(m_p[:, :, 0, 0])
            outs["mcp"].append(u_p[:, :wc].reshape(nb, t, wc)[:, t - (CONV_W - 1):])
            u_s = _mm([xs], w_in_odd, o, 0, 2 * wc, nbs, tn, P)
            g_s = _mm([xs], w_gate, o, 0, 2 * mh, nbs, 2 * mh, P)
            buf = state_mlstm_conv[o]
            xc_s, q_s, k_s, v_s = _ml_pre_step(u_s, wc, o, jnp.swapaxes(buf, 0, 1), ml_conv_w,
                                               ml_conv_b, wqk, wv, kscale, P)
            pre_s, c_s, nn_s, m_s = _ml_step(
                q_s, k_s, v_s, u_s, wc // dh, xc_s, g_s[:, :mh] + ml_b_i[o], g_s[:, mh:] + ml_b_f[o],
                o, state_mlstm_m[o], state_mlstm_c, state_mlstm_n, ml_gn_g, ml_skip, mh, dh, P)
            mix_s = _mm([pre_s], w_out_odd, o, 0, d, nbs, tn, P)
            outs["cs"].append(c_s)
            outs["ns"].append(nn_s)
            outs["ms"].append(m_s)
            outs["mcs"].append(jnp.concatenate([buf[:, 1:], u_s[:, None, :wc]], axis=1))

        xp = _ln_add(xp, mix_p, ln1_g, ln1_b, l, alpha, tm_ln)
        xs = _ln_add(xs, mix_s, ln1_g, ln1_b, l, alpha, nbs)
        idx_p, gate_p, pos_p, cnt_p = _router(xp, w_router_t, b_router, tm_p, tm_p, P)
        xs_pad = jnp.pad(xs, ((0, ROUTER_SAMPLE_TILE - nbs), (0, 0)))
        idx_s, gate_s, pos_s, cnt_s = _router(xs_pad, w_router_t, b_router, ROUTER_SAMPLE_TILE, nbs, P)
        idx = jnp.concatenate([idx_p, idx_s[:, :nbs]], axis=1)
        pos = jnp.concatenate([pos_p, pos_s[:, :nbs]], axis=1)
        gates = jnp.concatenate([gate_p, gate_s[:, :nbs]], axis=1).T
        cnt = jnp.concatenate([cnt_p, cnt_s], axis=0)
        dest, blk_e, blk_n = _dispatch_plan(idx, pos, cnt, tile_rows, n_exp, bm, n_blocks)
        xb = _dispatch(dest, xp, xs, n_blocks * bm, tm_ln)
        yb = _experts(xb, blk_e, blk_n, l, w_e1, w_e3, w_e2, bm, MOE_HIDDEN_SPLIT, pp)
        xp = _combine_ln(dest, xp, gates, yb, 0, ln2_g, ln2_b, l, alpha, tm_ln)
        xs = _combine_ln(dest, xs, gates, yb, n_p, ln2_g, ln2_b, l, alpha, nbs)

    st = lambda k: jnp.stack(outs[k])
    return (xp.reshape(nb, t, d), xs.reshape(nbs, 1, d),
            st("kp"), st("vp"), st("ks"), st("vs"), st("hp"), st("hs"), st("rcp"), st("rcs"),
            st("cp"), st("cs"), st("np"), st("ns"), st("mp"), st("ms"), st("mcp"), st("mcs"))
```

```python
import functools
import math

import jax
import jax.numpy as jnp
from jax import lax
from jax.experimental import pallas as pl
from jax.experimental.pallas import tpu as pltpu

F32 = jnp.float32
I32 = jnp.int32
BF16 = jnp.bfloat16

V7X_LANES = 128
V7X_SUBLANES = 8
VMEM_LIMIT_BYTES = 60 * 1024 * 1024

CONV_W = 4
RG_C = 8.0
ROPE_THETA = 10000.0
N_GROUPS = 4
TOP_K = 2
LN_EPS = 1e-5
NEG_BIG = -1e30

PRECISE_PASSES = 3
LAST_LAYER_PASSES = 1

NN = (((1,), (0,)), ((), ()))
NT = (((1,), (1,)), ((), ()))
TN = (((0,), (0,)), ((), ()))


def _cparams(sem):
    return pltpu.CompilerParams(dimension_semantics=sem, vmem_limit_bytes=VMEM_LIMIT_BYTES)


def _split(x):
    hi = x.astype(BF16)
    lo = (x - hi.astype(F32)).astype(BF16)
    return hi, lo


def _dg(a, b, dims):
    return lax.dot_general(a, b, dims, preferred_element_type=F32)


def _mxdot(a, b, passes, dims=NN):
    if passes == 1:
        return _dg(a.astype(BF16), b.astype(BF16), dims)
    ah, al = _split(a)
    bh, bl = _split(b)
    return _dg(ah, bh, dims) + (_dg(al, bh, dims) + _dg(ah, bl, dims))


def _mxdot_pre(a, bh, bl, passes, dims=NN):
    if passes == 1:
        return _dg(a.astype(BF16), bh, dims)
    ah, al = _split(a)
    return _dg(ah, bh, dims) + (_dg(al, bh, dims) + _dg(ah, bl, dims))


def _store_split(w, wh_ref, wl_ref, passes):
    if passes == 1:
        wh_ref[...] = w.astype(BF16)
    else:
        h, l = _split(w)
        wh_ref[...] = h
        wl_ref[...] = l


def _r3(a):
    return a.reshape(a.shape[0], 1, a.shape[1])


def _lo_shape(shape, passes):
    return shape if passes > 1 else (V7X_SUBLANES, V7X_LANES)


def _mm_kernel(*refs, k_spans, passes):
    n_x = len(k_spans)
    x_refs, w_ref, o_ref, wh, wl = refs[:n_x], refs[n_x], refs[n_x + 1], refs[n_x + 2], refs[n_x + 3]

    @pl.when(pl.program_id(1) == 0)
    def _():
        _store_split(w_ref[0], wh, wl, passes)

    acc = None
    for x_ref, (k0, kn) in zip(x_refs, k_spans):
        bl = wl[k0:k0 + kn, :] if passes > 1 else None
        p = _mxdot_pre(x_ref[...], wh[k0:k0 + kn, :], bl, passes)
        acc = p if acc is None else acc + p
    o_ref[...] = acc


def _mm(xs, w, layer, col0, ncols, tm, tn, passes):
    m = xs[0].shape[0]
    k = w.shape[1]
    spans, k0 = [], 0
    for x in xs:
        spans.append((k0, x.shape[1]))
        k0 += x.shape[1]
    assert k0 == k and m % tm == 0 and ncols % tn == 0 and col0 % tn == 0
    cb0 = col0 // tn
    in_specs = [pl.BlockSpec((tm, x.shape[1]), lambda j, i: (i, 0)) for x in xs]
    in_specs.append(pl.BlockSpec((1, k, tn), lambda j, i: (layer, 0, j + cb0)))
    return pl.pallas_call(
        functools.partial(_mm_kernel, k_spans=tuple(spans), passes=passes),
        out_shape=jax.ShapeDtypeStruct((m, ncols), F32),
        grid=(ncols // tn, m // tm),
        in_specs=in_specs,
        out_specs=pl.BlockSpec((tm, tn), lambda j, i: (i, j)),
        scratch_shapes=[pltpu.VMEM((k, tn), BF16), pltpu.VMEM(_lo_shape((k, tn), passes), BF16)],
        compiler_params=_cparams(("arbitrary", "arbitrary")),
        name="mm",
    )(*xs, w)


def _layer_norm_rows(y, g, b):
    mu = jnp.mean(y, axis=-1, keepdims=True)
    yc = y - mu
    var = jnp.mean(yc * yc, axis=-1, keepdims=True)
    return yc * lax.rsqrt(var + LN_EPS) * g + b


def _ln_add_kernel(x_ref, mix_ref, g_ref, b_ref, o_ref, *, alpha):
    y = alpha * x_ref[...] + mix_ref[...]
    o_ref[...] = _layer_norm_rows(y, g_ref[...], b_ref[...])


def _ln_add(x, mix, g, b, layer, alpha, tm):
    m, d = x.shape
    row = pl.BlockSpec((tm, d), lambda i: (i, 0))
    vec = pl.BlockSpec((None, 1, d), lambda i: (layer, 0, 0))
    return pl.pallas_call(
        functools.partial(_ln_add_kernel, alpha=alpha),
        out_shape=jax.ShapeDtypeStruct((m, d), F32),
        grid=(m // tm,),
        in_specs=[row, row, vec, vec],
        out_specs=row,
        compiler_params=_cparams(("arbitrary",)),
        name="ln_add",
    )(x, mix, _r3(g), _r3(b))


def _conv_rows(x, tail, cw, cb):
    t = x.shape[0]
    ext = jnp.concatenate([tail, x], axis=0)
    y = cb + cw[CONV_W - 1:CONV_W] * x
    for j in range(CONV_W - 1):
        off = V7X_SUBLANES - (CONV_W - 1) + j
        y = y + cw[j:j + 1] * ext[off:off + t]
    return y


def _scan_rows(a, b):
    t = a.shape[0]
    row = lax.broadcasted_iota(I32, a.shape, 0)
    d = 1
    while d < t:
        keep = row >= d
        a_sh = jnp.where(keep, pltpu.roll(a, d, 0), 1.0)
        b_sh = jnp.where(keep, pltpu.roll(b, d, 0), 0.0)
        b = a * b_sh + b
        a = a * a_sh
        d *= 2
    return a, b


def _rg_gates(xc, wr, br, wi, bi, lam, passes):
    r = jax.nn.sigmoid(_mxdot(xc, wr, passes) + br)
    i = jax.nn.sigmoid(_mxdot(xc, wi, passes) + bi)
    log_a = -RG_C * jax.nn.softplus(-lam) * r
    a = jnp.exp(log_a)
    bt = jnp.sqrt(jnp.tanh(-log_a) * (a * a + 1.0)) * (i * xc)
    return a, bt


def _rglru_kernel(xa_ref, ga_ref, cw_ref, cb_ref, wr_ref, br_ref, wi_ref, bi_ref, lam_ref,
                  y_ref, hl_ref, tail_scr, h_scr, *, tc, nblk, bs, passes):
    @pl.when(pl.program_id(1) == 0)
    def _():
        tail_scr[...] = jnp.zeros_like(tail_scr)
        h_scr[...] = jnp.zeros_like(h_scr)

    for n in range(nblk):
        sl = slice(n * bs, (n + 1) * bs)
        x = xa_ref[:, sl]
        xc = _conv_rows(x, tail_scr[:, sl], cw_ref[0, :, sl], cb_ref[:, sl])
        a, bt = _rg_gates(xc, wr_ref[0, n], br_ref[:, sl], wi_ref[0, n], bi_ref[:, sl], lam_ref[:, sl],
                          passes)
        pa, hb = _scan_rows(a, bt)
        h = hb + pa * h_scr[0:1, sl]
        y_ref[:, sl] = h * jax.nn.gelu(ga_ref[:, sl])
        hl = h[tc - 1:tc, :]
        h_scr[:, sl] = jnp.broadcast_to(hl, (V7X_SUBLANES, bs))
        hl_ref[0, :, sl] = hl
        tail_scr[:, sl] = x[tc - V7X_SUBLANES:, :]


def _rglru_prompt(u, nb, t, wa, layer, conv_w, conv_b, w_r, b_r, w_i, b_i, lam, tc, passes):
    nblk, bs = w_r.shape[1], w_r.shape[2]
    nc = t // tc
    vec = pl.BlockSpec((None, 1, wa), lambda b, c: (layer, 0, 0))
    wspec = pl.BlockSpec((1, nblk, bs, bs), lambda b, c: (layer, 0, 0, 0))
    y, hl = pl.pallas_call(
        functools.partial(_rglru_kernel, tc=tc, nblk=nblk, bs=bs, passes=passes),
        out_shape=(jax.ShapeDtypeStruct((nb * t, wa), F32), jax.ShapeDtypeStruct((nb, 1, wa), F32)),
        grid=(nb, nc),
        in_specs=[pl.BlockSpec((tc, wa), lambda b, c: (b * nc + c, 0)),
                  pl.BlockSpec((tc, wa), lambda b, c: (b * nc + c, 1)),
                  pl.BlockSpec((1, CONV_W, wa), lambda b, c: (layer, 0, 0)), vec, wspec, vec, wspec,
                  vec, vec],
        out_specs=(pl.BlockSpec((tc, wa), lambda b, c: (b * nc + c, 0)),
                   pl.BlockSpec((1, 1, wa), lambda b, c: (b, 0, 0))),
        scratch_shapes=[pltpu.VMEM((V7X_SUBLANES, wa), F32), pltpu.VMEM((V7X_SUBLANES, wa), F32)],
        compiler_params=_cparams(("arbitrary", "arbitrary")),
        name="rglru_prompt",
    )(u, u, conv_w, _r3(conv_b), w_r, _r3(b_r), w_i, _r3(b_i), _r3(lam))
    return y, hl[:, 0]


def _rglru_step_kernel(xa_ref, ga_ref, buf_ref, h0_ref, cw_ref, cb_ref, wr_ref, br_ref, wi_ref,
                       bi_ref, lam_ref, y_ref, h_ref, *, nblk, bs, passes):
    for n in range(nblk):
        sl = slice(n * bs, (n + 1) * bs)
        x = xa_ref[:, sl]
        xc = cb_ref[:, sl] + cw_ref[0, CONV_W - 1:CONV_W, sl] * x
        for j in range(CONV_W - 1):
            xc = xc + cw_ref[0, j:j + 1, sl] * buf_ref[j, :, sl]
        a, bt = _rg_gates(xc, wr_ref[0, n], br_ref[:, sl], wi_ref[0, n], bi_ref[:, sl], lam_ref[:, sl],
                          passes)
        h = a * h0_ref[0, :, sl] + bt
        h_ref[:, sl] = h
        y_ref[:, sl] = h * jax.nn.gelu(ga_ref[:, sl])


def _rglru_step(u, wa, layer, buf, h0, conv_w, conv_b, w_r, b_r, w_i, b_i, lam, passes):
    nblk, bs = w_r.shape[1], w_r.shape[2]
    nb = u.shape[0]
    vec = pl.BlockSpec((None, 1, wa), lambda i: (layer, 0, 0))
    wspec = pl.BlockSpec((1, nblk, bs, bs), lambda i: (layer, 0, 0, 0))
    rows = pl.BlockSpec((nb, wa), lambda i: (0, 0))
    return pl.pallas_call(
        functools.partial(_rglru_step_kernel, nblk=nblk, bs=bs, passes=passes),
        out_shape=(jax.ShapeDtypeStruct((nb, wa), F32), jax.ShapeDtypeStruct((nb, wa), F32)),
        grid=(1,),
        in_specs=[rows, pl.BlockSpec((nb, wa), lambda i: (0, 1)),
                  pl.BlockSpec((CONV_W - 1, nb, wa), lambda i: (0, 0, 0)),
                  pl.BlockSpec((1, nb, wa), lambda i: (layer, 0, 0)),
                  pl.BlockSpec((1, CONV_W, wa), lambda i: (layer, 0, 0)), vec, wspec, vec, wspec, vec, vec],
        out_specs=(rows, rows),
        compiler_params=_cparams(("arbitrary",)),
        name="rglru_step",
    )(u, u, buf, h0, conv_w, _r3(conv_b), w_r, _r3(b_r), w_i, _r3(b_i), _r3(lam))


def _rope_kernel(q_ref, k_ref, cos_ref, sin_ref, qo_ref, ko_ref, *, nchunk, hd, qscale):
    cos = cos_ref[...]
    sin = sin_ref[...]
    for j in range(nchunk):
        sl = slice(j * hd, (j + 1) * hd)
        q = q_ref[:, sl]
        k = k_ref[:, sl]
        qo_ref[:, sl] = (q * cos + pltpu.roll(q, hd // 2, 1) * sin) * qscale
        ko_ref[:, sl] = k * cos + pltpu.roll(k, hd // 2, 1) * sin


def _rope(u, qcol, kcol, width, cos2, sin2, hd, tm):
    m = u.shape[0]
    nt = cos2.shape[0] // tm
    return pl.pallas_call(
        functools.partial(_rope_kernel, nchunk=width // hd, hd=hd, qscale=hd ** -0.5),
        out_shape=(jax.ShapeDtypeStruct((m, width), F32), jax.ShapeDtypeStruct((m, width), F32)),
        grid=(m // tm,),
        in_specs=[pl.BlockSpec((tm, width), lambda i: (i, qcol)),
                  pl.BlockSpec((tm, width), lambda i: (i, kcol)),
                  pl.BlockSpec((tm, hd), lambda i: (i % nt, 0)),
                  pl.BlockSpec((tm, hd), lambda i: (i % nt, 0))],
        out_specs=(pl.BlockSpec((tm, width), lambda i: (i, 0)),
                   pl.BlockSpec((tm, width), lambda i: (i, 0))),
        compiler_params=_cparams(("arbitrary",)),
        name="rope",
    )(u, u, cos2, sin2)


def _rope_tables(pos, hd):
    half = hd // 2
    inv = ROPE_THETA ** (-jnp.arange(half, dtype=F32) / half)
    ang = pos.astype(F32)[:, None] * inv[None, :]
    cos, sin = jnp.cos(ang), jnp.sin(ang)
    return jnp.concatenate([cos, cos], -1), jnp.concatenate([-sin, sin], -1)


def _diff_lambda(lq1, lk1, lq2, lk2, lam_init):
    return (jnp.exp(jnp.sum(lq1 * lk1, axis=-1, keepdims=True))
            - jnp.exp(jnp.sum(lq2 * lk2, axis=-1, keepdims=True)) + lam_init)


def _sub_norm(o, subg, lam_init):
    o = o * lax.rsqrt(jnp.mean(o * o, axis=-1, keepdims=True) + LN_EPS) * subg
    return o * (1.0 - lam_init)


def _attn_kernel(q_ref, k_ref, v_ref, lq1_ref, lk1_ref, lq2_ref, lk2_ref, subg_ref, o_ref,
                 m_scr, l_scr, acc_scr, *, bq, bk, hd, lam_init, passes):
    qi = pl.program_id(2)
    ki = pl.program_id(3)

    @pl.when(ki == 0)
    def _():
        m_scr[...] = jnp.full_like(m_scr, NEG_BIG)
        l_scr[...] = jnp.zeros_like(l_scr)
        acc_scr[...] = jnp.zeros_like(acc_scr)

    @pl.when(ki * bk <= qi * bq + (bq - 1))
    def _():
        q = q_ref[...]
        k = k_ref[...]
        if passes == 1:
            vh, vl = v_ref[...].astype(BF16), None
        else:
            vh, vl = _split(v_ref[...])
        rows = qi * bq + lax.broadcasted_iota(I32, (bq, bk), 0)
        cols = ki * bk + lax.broadcasted_iota(I32, (bq, bk), 1)
        mask = cols <= rows
        for c in range(2):
            s = _mxdot(q[:, c * hd:(c + 1) * hd], k[:, c * hd:(c + 1) * hd], passes, NT)
            s = jnp.where(mask, s, NEG_BIG)
            m_prev = m_scr[c]
            m_new = jnp.maximum(m_prev, jnp.max(s, axis=1, keepdims=True))
            alpha = jnp.exp(m_prev - m_new)
            p = jnp.exp(s - m_new)
            l_scr[c] = alpha * l_scr[c] + jnp.sum(p, axis=1, keepdims=True)
            acc_scr[c] = alpha * acc_scr[c] + _mxdot_pre(p, vh, vl, passes)
            m_scr[c] = m_new

    @pl.when(ki == pl.num_programs(3) - 1)
    def _():
        lam = _diff_lambda(lq1_ref[...], lk1_ref[...], lq2_ref[...], lk2_ref[...], lam_init)
        o = acc_scr[0] / l_scr[0] - lam * (acc_scr[1] / l_scr[1])
        o_ref[...] = _sub_norm(o, subg_ref[...], lam_init)


def _attn_prompt(qr, kr, u, vcol, nb, t, nh, hd, layer, lq1, lk1, lq2, lk2, subg, lam_init, bq, bk,
                 passes):
    vd = 2 * hd
    nq, nk = t // bq, t // bk

    def kv_row(b, qi, ki):
        return b * nk + jnp.minimum(ki, (qi * bq + bq - 1) // bk)

    vec = pl.BlockSpec((None, 1, hd), lambda b, h, qi, ki: (layer, 0, 0))
    return pl.pallas_call(
        functools.partial(_attn_kernel, bq=bq, bk=bk, hd=hd, lam_init=lam_init, passes=passes),
        out_shape=jax.ShapeDtypeStruct((nb * t, nh * vd), F32),
        grid=(nb, nh, nq, nk),
        in_specs=[pl.BlockSpec((bq, vd), lambda b, h, qi, ki: (b * nq + qi, h)),
                  pl.BlockSpec((bk, vd), lambda b, h, qi, ki: (kv_row(b, qi, ki), h)),
                  pl.BlockSpec((bk, vd), lambda b, h, qi, ki: (kv_row(b, qi, ki), vcol + h)),
                  vec, vec, vec, vec,
                  pl.BlockSpec((None, 1, vd), lambda b, h, qi, ki: (layer, 0, 0))],
        out_specs=pl.BlockSpec((bq, vd), lambda b, h, qi, ki: (b * nq + qi, h)),
        scratch_shapes=[pltpu.VMEM((2, bq, 1), F32), pltpu.VMEM((2, bq, 1), F32),
                        pltpu.VMEM((2, bq, vd), F32)],
        compiler_params=_cparams(("arbitrary", "arbitrary", "arbitrary", "arbitrary")),
        name="attn_prompt",
    )(qr, kr, u, _r3(lq1), _r3(lk1), _r3(lq2), _r3(lk2), _r3(subg))


def _dec_attn_kernel(pt_ref, q_ref, kn_ref, vn_ref, kc_ref, vc_ref, lq1_ref, lk1_ref, lq2_ref,
                     lk2_ref, subg_ref, o_ref, m_scr, l_scr, acc_scr, *, hd, lam_init):
    del pt_ref
    j = pl.program_id(1)
    q = q_ref[0]

    @pl.when(j == 0)
    def _():
        m_scr[...] = jnp.full_like(m_scr, NEG_BIG)
        l_scr[...] = jnp.zeros_like(l_scr)
        acc_scr[...] = jnp.zeros_like(acc_scr)

    nh, vd = q.shape
    reps = V7X_SUBLANES // nh
    tiles = kc_ref.shape[2] // V7X_SUBLANES
    fold = lambda a, op: functools.reduce(op, [a[r * nh:(r + 1) * nh] for r in range(reps)])
    tile_rows = lambda a: jnp.concatenate([a] * reps, axis=0)
    k3 = kc_ref[0, 0].reshape(tiles, V7X_SUBLANES, vd)
    v3 = vc_ref[0, 0].reshape(tiles, V7X_SUBLANES, vd)
    prod = k3 * tile_rows(q)[None]
    for c in range(2):
        s = jnp.sum(prod[:, :, c * hd:(c + 1) * hd], axis=-1, keepdims=True)
        m_prev = m_scr[c]
        m_new = jnp.maximum(m_prev, fold(jnp.max(s, axis=0), jnp.maximum))
        alpha = jnp.exp(m_prev - m_new)
        p = jnp.exp(s - tile_rows(m_new)[None])
        l_scr[c] = alpha * l_scr[c] + fold(jnp.sum(p, axis=0), jnp.add)
        acc_scr[c] = alpha * acc_scr[c] + fold(jnp.sum(p * v3, axis=0), jnp.add)
        m_scr[c] = m_new

    @pl.when(j == pl.num_programs(1) - 1)
    def _():
        kn = kn_ref[0]
        vn = vn_ref[0]
        outs = []
        for c in range(2):
            cs = slice(c * hd, (c + 1) * hd)
            s = jnp.sum(kn[:, cs] * q[:, cs], axis=-1, keepdims=True)
            m_prev = m_scr[c]
            m_new = jnp.maximum(m_prev, s)
            alpha = jnp.exp(m_prev - m_new)
            p = jnp.exp(s - m_new)
            l = alpha * l_scr[c] + p
            outs.append((alpha * acc_scr[c] + p * vn) / l)
        lam = _diff_lambda(lq1_ref[...], lk1_ref[...], lq2_ref[...], lk2_ref[...], lam_init)
        o_ref[0] = _sub_norm(outs[0] - lam * outs[1], subg_ref[...], lam_init)


def _attn_decode(page_table, layer, qs, ks, vs, cache_k, cache_v, nh, hd, lq1, lk1, lq2, lk2, subg,
                 lam_init):
    nb, n_pages = page_table.shape
    page = cache_k.shape[2]
    vd = 2 * hd
    row = pl.BlockSpec((1, nh, vd), lambda b, j, pt: (b, 0, 0))
    pg = pl.BlockSpec((1, 1, page * nh, vd), lambda b, j, pt: (layer, pt[b * n_pages + j], 0, 0))
    cache_k = cache_k.reshape(cache_k.shape[:2] + (page * nh, vd))
    cache_v = cache_v.reshape(cache_v.shape[:2] + (page * nh, vd))
    vec = pl.BlockSpec((None, 1, hd), lambda b, j, pt: (layer, 0, 0))
    grid_spec = pltpu.PrefetchScalarGridSpec(
        num_scalar_prefetch=1,
        grid=(nb, n_pages),
        in_specs=[row, row, row, pg, pg, vec, vec, vec, vec,
                  pl.BlockSpec((None, 1, vd), lambda b, j, pt: (layer, 0, 0))],
        out_specs=row,
        scratch_shapes=[pltpu.VMEM((2, nh, 1), F32), pltpu.VMEM((2, nh, 1), F32),
                        pltpu.VMEM((2, nh, vd), F32)],
    )
    out = pl.pallas_call(
        functools.partial(_dec_attn_kernel, hd=hd, lam_init=lam_init),
        out_shape=jax.ShapeDtypeStruct((nb, nh, vd), F32),
        grid_spec=grid_spec,
        compiler_params=_cparams(("arbitrary", "arbitrary")),
        name="attn_decode",
    )(page_table.reshape(-1), qs.reshape(nb, nh, vd), ks.reshape(nb, nh, vd), vs.reshape(nb, nh, vd),
      cache_k, cache_v, _r3(lq1), _r3(lk1), _r3(lq2), _r3(lk2), _r3(subg))
    return out.reshape(nb, nh * vd)


def _expand_block_diag(w, lanes):
    n, bs, _ = w.shape
    g = lanes // bs
    w4 = w.reshape(n // g, g, bs, bs)
    eye = jnp.eye(g, dtype=w.dtype)
    dense = w4[:, :, :, None, :] * eye[None, :, None, :, None]
    return dense.reshape(n // g, lanes, lanes)


def _ml_qkv(xc, xm, wqk_ref, wv_ref, q_ref, k_ref, v_ref, kscale, passes):
    lanes = wv_ref.shape[1]
    for j in range(wv_ref.shape[0]):
        sl = slice(j * lanes, (j + 1) * lanes)
        qk = _mxdot(xc[:, sl], wqk_ref[j], passes)
        q_ref[:, sl] = qk[:, :lanes]
        k_ref[:, sl] = qk[:, lanes:] * kscale
        v_ref[:, sl] = _mxdot(xm[:, sl], wv_ref[j], passes)


def _ml_pre_kernel(xm_ref, cw_ref, cb_ref, wqk_ref, wv_ref, xc_ref, q_ref, k_ref, v_ref, tail_scr,
                   *, tc, kscale, passes):
    @pl.when(pl.program_id(1) == 0)
    def _():
        tail_scr[...] = jnp.zeros_like(tail_scr)

    x = xm_ref[...]
    xc = jax.nn.silu(_conv_rows(x, tail_scr[...], cw_ref[0], cb_ref[...]))
    tail_scr[...] = x[tc - V7X_SUBLANES:, :]
    xc_ref[...] = xc
    _ml_qkv(xc, x, wqk_ref, wv_ref, q_ref, k_ref, v_ref, kscale, passes)


def _ml_pre_prompt(u, nb, t, wc, layer, conv_w, conv_b, wqk, wv, kscale, tc, passes):
    nc = t // tc
    nt, lanes = wv.shape[0], wv.shape[1]
    rows = pl.BlockSpec((tc, wc), lambda b, c: (b * nc + c, 0))
    return pl.pallas_call(
        functools.partial(_ml_pre_kernel, tc=tc, kscale=kscale, passes=passes),
        out_shape=(jax.ShapeDtypeStruct((nb * t, wc), F32),) * 4,
        grid=(nb, nc),
        in_specs=[rows, pl.BlockSpec((1, CONV_W, wc), lambda b, c: (layer, 0, 0)),
                  pl.BlockSpec((None, 1, wc), lambda b, c: (layer, 0, 0)),
                  pl.BlockSpec((nt, lanes, 2 * lanes), lambda b, c: (0, 0, 0)),
                  pl.BlockSpec((nt, lanes, lanes), lambda b, c: (0, 0, 0))],
        out_specs=(rows, rows, rows, rows),
        scratch_shapes=[pltpu.VMEM((V7X_SUBLANES, wc), F32)],
        compiler_params=_cparams(("arbitrary", "arbitrary")),
        name="mlstm_pre_prompt",
    )(u, conv_w, _r3(conv_b), wqk, wv)


def _ml_pre_step_kernel(xm_ref, buf_ref, cw_ref, cb_ref, wqk_ref, wv_ref, xc_ref, q_ref, k_ref,
                        v_ref, *, kscale, passes):
    x = xm_ref[...]
    xc = cb_ref[...] + cw_ref[0, CONV_W - 1:CONV_W, :] * x
    for j in range(CONV_W - 1):
        xc = xc + cw_ref[0, j:j + 1, :] * buf_ref[j]
    xc = jax.nn.silu(xc)
    xc_ref[...] = xc
    _ml_qkv(xc, x, wqk_ref, wv_ref, q_ref, k_ref, v_ref, kscale, passes)


def _ml_pre_step(u, wc, layer, buf, conv_w, conv_b, wqk, wv, kscale, passes):
    nb = u.shape[0]
    nt, lanes = wv.shape[0], wv.shape[1]
    rows = pl.BlockSpec((nb, wc), lambda i: (0, 0))
    return pl.pallas_call(
        functools.partial(_ml_pre_step_kernel, kscale=kscale, passes=passes),
        out_shape=(jax.ShapeDtypeStruct((nb, wc), F32),) * 4,
        grid=(1,),
        in_specs=[rows, pl.BlockSpec((CONV_W - 1, nb, wc), lambda i: (0, 0, 0)),
                  pl.BlockSpec((1, CONV_W, wc), lambda i: (layer, 0, 0)),
                  pl.BlockSpec((None, 1, wc), lambda i: (layer, 0, 0)),
                  pl.BlockSpec((nt, lanes, 2 * lanes), lambda i: (0, 0, 0)),
                  pl.BlockSpec((nt, lanes, lanes), lambda i: (0, 0, 0))],
        out_specs=(rows, rows, rows, rows),
        compiler_params=_cparams(("arbitrary",)),
        name="mlstm_pre_step",
    )(u, buf, conv_w, _r3(conv_b), wqk, wv)


def _ml_out(h, z, xc, gng, skip):
    mu = jnp.mean(h, axis=-1, keepdims=True)
    hc = h - mu
    var = jnp.mean(hc * hc, axis=-1, keepdims=True)
    hn = hc * lax.rsqrt(var + LN_EPS) * gng
    return jax.nn.sigmoid(z) * (hn + skip * xc)


def _ml_scan_kernel(q_ref, k_ref, v_ref, z_ref, xc_ref, grow_ref, gcol_ref, brow_ref, bcol_ref,
                    gng_ref, skip_ref, pre_ref, c_ref, n_ref, m_ref, c_scr, n_scr, m_scr,
                    *, L, nh, passes):
    hh = pl.program_id(1)
    ci = pl.program_id(2)

    @pl.when(ci == 0)
    def _():
        c_scr[...] = jnp.zeros_like(c_scr)
        n_scr[...] = jnp.zeros_like(n_scr)
        m_scr[...] = jnp.zeros_like(m_scr)

    g_rows = grow_ref[0] + bcol_ref[...]
    sub = lax.broadcasted_iota(I32, g_rows.shape, 0)
    li_row = jnp.sum(jnp.where(sub == hh, g_rows, 0.0), axis=0, keepdims=True)
    lf_row = jax.nn.log_sigmoid(jnp.sum(jnp.where(sub == hh + nh, g_rows, 0.0), axis=0, keepdims=True))
    g_cols = gcol_ref[...] + brow_ref[...]
    lane = lax.broadcasted_iota(I32, g_cols.shape, 1)
    li_col = jnp.sum(jnp.where(lane == hh, g_cols, 0.0), axis=1, keepdims=True)
    lf_col = jax.nn.log_sigmoid(jnp.sum(jnp.where(lane == hh + nh, g_cols, 0.0), axis=1, keepdims=True))

    tt = lax.broadcasted_iota(I32, (L, L), 0)
    ss = lax.broadcasted_iota(I32, (L, L), 1)
    causal = ss <= tt
    f_col = jnp.sum(jnp.where(causal, jnp.broadcast_to(lf_row, (L, L)), 0.0), axis=1, keepdims=True)
    f_row = jnp.sum(jnp.where(tt <= ss, jnp.broadcast_to(lf_col, (L, L)), 0.0), axis=0, keepdims=True)
    f_last = jnp.sum(lf_col, axis=0, keepdims=True)

    m0 = m_scr[0:1, 0:1]
    dm = jnp.where(causal, f_col - f_row + li_row, -jnp.inf)
    bt = f_col + m0
    m = jnp.maximum(bt, jnp.max(dm, axis=1, keepdims=True))
    wd = jnp.exp(dm - m)
    inter = jnp.exp(bt - m)

    q = q_ref[...]
    k = k_ref[...]
    v = v_ref[...]
    c0 = c_scr[...]
    n0 = n_scr[0:1, :]
    if passes == 1:
        vh, vl = v.astype(BF16), None
    else:
        vh, vl = _split(v)
    s = _mxdot(q, k, passes, NT) * wd
    num = _mxdot_pre(s, vh, vl, passes) + inter * _mxdot(q, c0, passes)
    den = jnp.sum(s, axis=1, keepdims=True) + inter * jnp.sum(q * n0, axis=1, keepdims=True)
    h = num / jnp.maximum(jnp.abs(den), jnp.exp(-m))

    m_last = m[L - 1:L, :]
    wk_col = jnp.exp(f_last - f_col + li_col - m_last)
    decay = jnp.exp(f_last + m0 - m_last)
    kw = k * wk_col
    c_new = decay * c0 + _mxdot_pre(kw, vh, vl, passes, TN)
    n_new = decay * n0 + jnp.sum(kw, axis=0, keepdims=True)
    c_scr[...] = c_new
    n_scr[...] = jnp.broadcast_to(n_new, n_scr.shape)
    m_scr[...] = jnp.broadcast_to(m_last, m_scr.shape)

    pre_ref[...] = _ml_out(h, z_ref[...], xc_ref[...], gng_ref[...], skip_ref[...])

    @pl.when(ci == pl.num_programs(2) - 1)
    def _():
        c_ref[0, 0] = c_new
        n_ref[0, 0] = n_new
        m_ref[0, 0] = jnp.broadcast_to(m_last, (1, V7X_LANES))


def _ml_scan_prompt(q, k, v, u, zcol, xc, g_rows, g_cols, b_rows, b_cols, layer, gng, skip, nb, t, nh,
                    dh, L, passes):
    nc = t // L
    wc = nh * dh
    blk = lambda col0: pl.BlockSpec((L, dh), lambda b, h, c: (b * nc + c, col0 + h))
    vec = pl.BlockSpec((None, 1, dh), lambda b, h, c: (layer, 0, h))
    return pl.pallas_call(
        functools.partial(_ml_scan_kernel, L=L, nh=nh, passes=passes),
        out_shape=(jax.ShapeDtypeStruct((nb * t, wc), F32),
                   jax.ShapeDtypeStruct((nb, nh, dh, dh), F32),
                   jax.ShapeDtypeStruct((nb, nh, 1, dh), F32),
                   jax.ShapeDtypeStruct((nb, nh, 1, V7X_LANES), F32)),
        grid=(nb, nh, nc),
        in_specs=[blk(0), blk(0), blk(0), blk(zcol), blk(0),
                  pl.BlockSpec((1, 2 * nh, L), lambda b, h, c: (b, 0, c)),
                  pl.BlockSpec((L, 2 * nh), lambda b, h, c: (b * nc + c, 0)),
                  pl.BlockSpec((1, 2 * nh), lambda b, h, c: (0, 0)),
                  pl.BlockSpec((2 * nh, 1), lambda b, h, c: (0, 0)),
                  vec, vec],
        out_specs=(blk(0),
                   pl.BlockSpec((1, 1, dh, dh), lambda b, h, c: (b, h, 0, 0)),
                   pl.BlockSpec((1, 1, 1, dh), lambda b, h, c: (b, h, 0, 0)),
                   pl.BlockSpec((1, 1, 1, V7X_LANES), lambda b, h, c: (b, h, 0, 0))),
        scratch_shapes=[pltpu.VMEM((dh, dh), F32), pltpu.VMEM((V7X_SUBLANES, dh), F32),
                        pltpu.VMEM((V7X_SUBLANES, V7X_LANES), F32)],
        compiler_params=_cparams(("arbitrary", "arbitrary", "arbitrary")),
        name="mlstm_scan_prompt",
    )(q, k, v, u, xc, g_rows, g_cols, b_rows, b_cols, _r3(gng), _r3(skip))


def _ml_step_kernel(q_ref, k_ref, v_ref, z_ref, xc_ref, li_ref, fg_ref, m0_ref, c0_ref, n0_ref,
                    gng_ref, skip_ref, pre_ref, c_ref, n_ref, m_ref, *, passes):
    q = q_ref[0]
    k = k_ref[0]
    v = v_ref[0]
    c0 = c0_ref[0, 0, 0]
    n0 = n0_ref[0, 0, 0]
    li = li_ref[0, 0][:, 0:1]
    lf = jax.nn.log_sigmoid(fg_ref[0, 0][:, 0:1])
    m0 = m0_ref[0, 0][:, 0:1]
    bt = lf + m0
    m = jnp.maximum(bt, li)
    wd = jnp.exp(li - m)
    inter = jnp.exp(bt - m)
    rows = V7X_SUBLANES
    dh = q.shape[1]
    first = lax.broadcasted_iota(I32, (rows, dh), 0) == 0
    q8 = jnp.where(first, jnp.broadcast_to(q, (rows, dh)), 0.0)
    k8 = jnp.where(first, jnp.broadcast_to(k, (rows, dh)), 0.0)
    v8 = jnp.where(first, jnp.broadcast_to(v, (rows, dh)), 0.0)
    s = jnp.sum(q * k, axis=1, keepdims=True) * wd
    qc = _mxdot(q8, c0, passes)[0:1, :]
    num = s * v + inter * qc
    den = s + inter * jnp.sum(q * n0, axis=1, keepdims=True)
    h = num / jnp.maximum(jnp.abs(den), jnp.exp(-m))
    outer = _mxdot(k8 * wd, v8, passes, TN)
    c_ref[0, 0] = inter * c0 + outer
    n_ref[0, 0] = inter * n0 + wd * k
    m_ref[0, 0] = jnp.broadcast_to(m, (1, V7X_LANES))
    pre_ref[0] = _ml_out(h, z_ref[0], xc_ref[0], gng_ref[...], skip_ref[...])


def _ml_step(q, k, v, u, zcol, xc, li, fg, layer, m0, c0, n0, gng, skip, nh, dh, passes):
    nb = q.shape[0]
    wc = nh * dh
    r3 = lambda a: a.reshape(nb, 1, a.shape[1])
    blk = lambda col0: pl.BlockSpec((1, 1, dh), lambda b, h: (b, 0, col0 + h))
    sc = pl.BlockSpec((1, 1, 1, V7X_LANES), lambda b, h: (b, h, 0, 0))
    vec = pl.BlockSpec((None, 1, dh), lambda b, h: (layer, 0, h))
    rep = lambda a: jnp.broadcast_to(a[:, :, None, None], (nb, nh, 1, V7X_LANES))
    n0r = n0.reshape(n0.shape[0], nb, nh, 1, dh)
    pre, c, n, m = pl.pallas_call(
        functools.partial(_ml_step_kernel, passes=passes),
        out_shape=(jax.ShapeDtypeStruct((nb, 1, wc), F32),
                   jax.ShapeDtypeStruct((nb, nh, dh, dh), F32),
                   jax.ShapeDtypeStruct((nb, nh, 1, dh), F32),
                   jax.ShapeDtypeStruct((nb, nh, 1, V7X_LANES), F32)),
        grid=(nb, nh),
        in_specs=[blk(0), blk(0), blk(0), blk(zcol), blk(0), sc, sc, sc,
                  pl.BlockSpec((1, 1, 1, dh, dh), lambda b, h: (layer, b, h, 0, 0)),
                  pl.BlockSpec((1, 1, 1, 1, dh), lambda b, h: (layer, b, h, 0, 0)), vec, vec],
        out_specs=(blk(0),
                   pl.BlockSpec((1, 1, dh, dh), lambda b, h: (b, h, 0, 0)),
                   pl.BlockSpec((1, 1, 1, dh), lambda b, h: (b, h, 0, 0)), sc),
        compiler_params=_cparams(("arbitrary", "arbitrary")),
        name="mlstm_step",
    )(r3(q), r3(k), r3(v), r3(u), r3(xc), rep(li), rep(fg), rep(m0), c0, n0r, _r3(gng), _r3(skip))
    return pre.reshape(nb, wc), c, n[:, :, 0], m[:, :, 0, 0]


def _router_kernel(x_ref, wt_ref, b_ref, idx_ref, gate_ref, pos_ref, cnt_ref, *, n_exp, n_grp,
                   n_valid, passes):
    per = n_exp // n_grp
    tm = x_ref.shape[0]
    logits = _mxdot(wt_ref[...], x_ref[...], passes, NT)
    s_all = jax.nn.sigmoid(logits)
    sb_all = s_all + b_ref[...]
    s = [s_all[e:e + 1, :] for e in range(n_exp)]
    sb = [sb_all[e:e + 1, :] for e in range(n_exp)]

    def top2_sum(a, b, c, d):
        hi1, lo1 = jnp.maximum(a, b), jnp.minimum(a, b)
        hi2, lo2 = jnp.maximum(c, d), jnp.minimum(c, d)
        return jnp.maximum(hi1, hi2) + jnp.maximum(jnp.minimum(hi1, hi2), jnp.maximum(lo1, lo2))

    assert per == 4
    gscore = [top2_sum(*sb[g * per:(g + 1) * per]) for g in range(n_grp)]
    best, gi = gscore[0], jnp.zeros_like(gscore[0], dtype=I32)
    for g in range(1, n_grp):
        take = gscore[g] > best
        gi = jnp.where(take, g, gi)
        best = jnp.maximum(best, gscore[g])

    def pick(vals, j):
        out = vals[j]
        for g in range(1, n_grp):
            out = jnp.where(gi == g, vals[g * per + j], out)
        return out

    v = [pick(sb, j) for j in range(per)]
    w = [pick(s, j) for j in range(per)]
    rank = []
    for j in range(per):
        r = jnp.zeros_like(gi)
        for i in range(per):
            if i == j:
                continue
            ahead = (v[i] >= v[j]) if i < j else (v[i] > v[j])
            r = r + ahead.astype(I32)
        rank.append(r)
    loc0 = sum(jnp.where(rank[j] == 0, j, 0) for j in range(per))
    loc1 = sum(jnp.where(rank[j] == 1, j, 0) for j in range(per))
    w0 = sum(jnp.where(rank[j] == 0, w[j], 0.0) for j in range(per))
    w1 = sum(jnp.where(rank[j] == 1, w[j], 0.0) for j in range(per))
    tot = w0 + w1
    valid = lax.broadcasted_iota(I32, (1, tm), 1) < n_valid
    e0 = jnp.where(valid, gi * per + loc0, -1)
    e1 = jnp.where(valid, gi * per + loc1, -1)
    idx_ref[0:1, :] = e0
    idx_ref[1:2, :] = e1
    gate_ref[0:1, :] = w0 / tot
    gate_ref[1:2, :] = w1 / tot
    eio = lax.broadcasted_iota(I32, (n_exp, tm), 0)
    oh0 = eio == e0
    oh1 = eio == e1
    sel = jnp.where(oh0, 1.0, jnp.where(oh1, 1.0, 0.0))
    earlier = (lax.broadcasted_iota(I32, (tm, tm), 0) < lax.broadcasted_iota(I32, (tm, tm), 1))
    prefix = _dg(sel.astype(BF16), jnp.where(earlier, 1.0, 0.0).astype(BF16), NN)
    pos_ref[0:1, :] = jnp.sum(jnp.where(oh0, prefix, 0.0), axis=0, keepdims=True).astype(I32)
    pos_ref[1:2, :] = jnp.sum(jnp.where(oh1, prefix, 0.0), axis=0, keepdims=True).astype(I32)
    cnt_ref[0] = jnp.broadcast_to(jnp.sum(sel, axis=1, keepdims=True), (n_exp, V7X_LANES)).astype(I32)


def _router(x, w_router_t, b_router, tm, n_valid, passes):
    m, d = x.shape
    n_exp = w_router_t.shape[0]
    col = pl.BlockSpec((TOP_K, tm), lambda i: (0, i))
    idx, gate, pos, cnt = pl.pallas_call(
        functools.partial(_router_kernel, n_exp=n_exp, n_grp=N_GROUPS, n_valid=n_valid, passes=passes),
        out_shape=(jax.ShapeDtypeStruct((TOP_K, m), I32), jax.ShapeDtypeStruct((TOP_K, m), F32),
                   jax.ShapeDtypeStruct((TOP_K, m), I32),
                   jax.ShapeDtypeStruct((m // tm, n_exp, V7X_LANES), I32)),
        grid=(m // tm,),
        in_specs=[pl.BlockSpec((tm, d), lambda i: (i, 0)),
                  pl.BlockSpec((n_exp, d), lambda i: (0, 0)),
                  pl.BlockSpec((n_exp, 1), lambda i: (0, 0))],
        out_specs=(col, col, col, pl.BlockSpec((1, n_exp, V7X_LANES), lambda i: (i, 0, 0))),
        compiler_params=_cparams(("arbitrary",)),
        name="router",
    )(x, w_router_t, b_router.reshape(n_exp, 1))
    return idx, gate, pos, cnt[:, :, 0]


def _dispatch_plan(idx, pos, cnt, tile_rows, n_exp, bm, n_blocks):
    tile_off = jnp.cumsum(cnt, axis=0) - cnt
    counts = jnp.sum(cnt, axis=0)
    padded = (counts + bm - 1) // bm * bm
    pad_end = jnp.cumsum(padded)
    pad_start = pad_end - padded
    base = pad_start[None, :] + tile_off
    base_tok = jnp.concatenate([jnp.broadcast_to(base[i:i + 1], (r, n_exp))
                                for i, r in enumerate(tile_rows)], axis=0)
    eio = jnp.arange(n_exp, dtype=I32)
    dest = pos + jnp.sum(jnp.where(idx[:, :, None] == eio, base_tok[None], 0), axis=-1)
    blk_start = jnp.arange(n_blocks, dtype=I32) * bm
    blk_e = jnp.minimum(jnp.sum((blk_start[:, None] >= pad_end[None, :]).astype(I32), axis=1), n_exp - 1)
    blk_n = jnp.clip(counts[blk_e] - (blk_start - pad_start[blk_e]), 0, bm)
    used = blk_start < pad_end[-1]
    blk_n = jnp.where(used, blk_n, 0).astype(I32)
    last_e = jnp.max(jnp.where(used, blk_e, 0))
    blk_e = jnp.where(used, blk_e, last_e).astype(I32)
    return dest.astype(I32).reshape(-1), blk_e, blk_n


def _dispatch_kernel(dest, x_ref, *refs, tm, nt, tok0):
    xb_hbm, sem = refs[-2], refs[-1]
    base = tok0 + pl.program_id(0) * tm

    def body(r, c):
        for kk in range(TOP_K):
            pltpu.make_async_copy(x_ref.at[pl.ds(r, 1)], xb_hbm.at[pl.ds(dest[kk * nt + base + r], 1)],
                                  sem).start()
        return c
    lax.fori_loop(0, tm, body, 0, unroll=8)
    for kk in range(TOP_K):
        pltpu.make_async_copy(x_ref, xb_hbm.at[pl.ds(0, tm)], sem).wait()


def _dispatch(dest, x, tok0, nt, rows, tm, xb_prev=None):
    m, d = x.shape
    any_spec = pl.BlockSpec(memory_space=pl.ANY)
    extra = [] if xb_prev is None else [any_spec]
    grid_spec = pltpu.PrefetchScalarGridSpec(
        num_scalar_prefetch=1, grid=(m // tm,),
        in_specs=[pl.BlockSpec((tm, d), lambda i, dst: (i, 0))] + extra, out_specs=any_spec,
        scratch_shapes=[pltpu.SemaphoreType.DMA])
    args = (dest, x) if xb_prev is None else (dest, x, xb_prev)
    return pl.pallas_call(
        functools.partial(_dispatch_kernel, tm=tm, nt=nt, tok0=tok0),
        out_shape=jax.ShapeDtypeStruct((rows, d), F32),
        grid_spec=grid_spec,
        input_output_aliases={} if xb_prev is None else {2: 0},
        compiler_params=_cparams(("arbitrary",)),
        name="moe_dispatch",
    )(*args)


def _experts_kernel(blk_e, blk_n, *refs, bm, passes, has_prev):
    if has_prev:
        xb_ref, yprev_ref, w1_ref, w3_ref, w2_ref, y_ref, w1h, w1l, w3h, w3l, w2h, w2l = refs
    else:
        xb_ref, w1_ref, w3_ref, w2_ref, y_ref, w1h, w1l, w3h, w3l, w2h, w2l = refs
        yprev_ref = None
    i = pl.program_id(0)
    n = blk_n[i]

    @pl.when(n == 0)
    def _():
        y_ref[...] = jnp.zeros_like(y_ref)

    @pl.when(n > 0)
    def _():
        changed = jnp.logical_or(i == 0, blk_e[i] != blk_e[jnp.maximum(i - 1, 0)])

        @pl.when(changed)
        def _():
            _store_split(w1_ref[0, 0], w1h, w1l, passes)
            _store_split(w3_ref[0, 0], w3h, w3l, passes)
            _store_split(w2_ref[0, 0], w2h, w2l, passes)

        rows = lax.broadcasted_iota(I32, (bm, 1), 0)
        x = jnp.where(rows < n, xb_ref[...], 0.0)
        lo = (lambda r: r[...]) if passes > 1 else (lambda r: None)
        h1 = _mxdot_pre(x, w1h[...], lo(w1l), passes)
        h3 = _mxdot_pre(x, w3h[...], lo(w3l), passes)
        y = _mxdot_pre(jax.nn.silu(h1) * h3, w2h[...], lo(w2l), passes)
        y_ref[...] = y if yprev_ref is None else yprev_ref[...] + y


def _experts(xb, blk_e, blk_n, layer, w1, w3, w2, bm, n_split, passes):
    rows, d = xb.shape
    n_blocks = rows // bm
    f = w1.shape[3] // n_split
    y = None
    for j in range(n_split):
        row = pl.BlockSpec((bm, d), lambda i, be, bn: (i, 0))
        grid_spec = pltpu.PrefetchScalarGridSpec(
            num_scalar_prefetch=2,
            grid=(n_blocks,),
            in_specs=[row] * (1 if y is None else 2) + [
                pl.BlockSpec((1, 1, d, f), lambda i, be, bn, j=j: (layer, be[i], 0, j)),
                pl.BlockSpec((1, 1, d, f), lambda i, be, bn, j=j: (layer, be[i], 0, j)),
                pl.BlockSpec((1, 1, f, d), lambda i, be, bn, j=j: (layer, be[i], j, 0))],
            out_specs=row,
            scratch_shapes=[pltpu.VMEM((d, f), BF16), pltpu.VMEM(_lo_shape((d, f), passes), BF16),
                            pltpu.VMEM((d, f), BF16), pltpu.VMEM(_lo_shape((d, f), passes), BF16),
                            pltpu.VMEM((f, d), BF16), pltpu.VMEM(_lo_shape((f, d), passes), BF16)],
        )
        args = (xb,) if y is None else (xb, y)
        y = pl.pallas_call(
            functools.partial(_experts_kernel, bm=bm, passes=passes, has_prev=y is not None),
            out_shape=jax.ShapeDtypeStruct((rows, d), F32),
            grid_spec=grid_spec,
            compiler_params=_cparams(("arbitrary",)),
            name="moe_experts",
        )(blk_e, blk_n, *args, w1, w3, w2)
    return y


def _combine_ln_kernel(dest, x_ref, gate_ref, g_ref, b_ref, yb_hbm, o_ref, buf, sem, *, alpha, tm, nt,
                       tok0):
    i = pl.program_id(0)
    slot = i % 2

    def issue(tile, sl):
        def body(r, c):
            t = tok0 + tile * tm + r
            for kk in range(TOP_K):
                pltpu.make_async_copy(yb_hbm.at[pl.ds(dest[kk * nt + t], 1)],
                                      buf.at[sl, kk, pl.ds(r, 1)], sem.at[sl]).start()
            return c
        lax.fori_loop(0, tm, body, 0, unroll=8)

    @pl.when(i == 0)
    def _():
        issue(0, 0)

    @pl.when(i + 1 < pl.num_programs(0))
    def _():
        issue(i + 1, 1 - slot)

    def wait_body(r, c):
        pltpu.make_async_copy(yb_hbm.at[pl.ds(0, 1)], buf.at[slot, 0, pl.ds(0, 1)], sem.at[slot]).wait()
        return c
    lax.fori_loop(0, TOP_K * tm, wait_body, 0)

    y = alpha * x_ref[...] + gate_ref[:, 0:1] * buf[slot, 0] + gate_ref[:, 1:2] * buf[slot, 1]
    o_ref[...] = _layer_norm_rows(y, g_ref[...], b_ref[...])


def _combine_ln(dest, x, gates, yb, tok0, g, b, layer, alpha, tm):
    m, d = x.shape
    nt = gates.shape[0]
    tb0 = tok0 // tm
    assert tok0 % tm == 0 and m % tm == 0
    grid_spec = pltpu.PrefetchScalarGridSpec(
        num_scalar_prefetch=1,
        grid=(m // tm,),
        in_specs=[pl.BlockSpec((tm, d), lambda i, dst: (i, 0)),
                  pl.BlockSpec((tm, TOP_K), lambda i, dst: (i + tb0, 0)),
                  pl.BlockSpec((None, 1, d), lambda i, dst: (layer, 0, 0)),
                  pl.BlockSpec((None, 1, d), lambda i, dst: (layer, 0, 0)),
                  pl.BlockSpec(memory_space=pl.ANY)],
        out_specs=pl.BlockSpec((tm, d), lambda i, dst: (i, 0)),
        scratch_shapes=[pltpu.VMEM((2, TOP_K, tm, d), F32), pltpu.SemaphoreType.DMA((2,))],
    )
    return pl.pallas_call(
        functools.partial(_combine_ln_kernel, alpha=alpha, tm=tm, nt=nt, tok0=tok0),
        out_shape=jax.ShapeDtypeStruct((m, d), F32),
        grid_spec=grid_spec,
        compiler_params=_cparams(("arbitrary",)),
        name="moe_combine_ln",
    )(dest, x, gates, _r3(g), _r3(b), yb)


def _tile(n, want):
    t = min(n, want)
    while n % t:
        t //= 2
    return t


MOE_BLOCK_ROWS = 256
MOE_HIDDEN_SPLIT = 2
ROUTER_SAMPLE_TILE = 128


def kernel(x_prompt, x_sample, cache_k, cache_v, state_rglru_h, state_rglru_conv, state_mlstm_c, state_mlstm_n, state_mlstm_m, state_mlstm_conv, page_table, ln1_g, ln1_b, ln2_g, ln2_b, w_in_even, w_out_even, rg_conv_w, rg_conv_b, rg_w_r, rg_b_r, rg_w_i, rg_b_i, rg_lambda, da_lam_q1, da_lam_k1, da_lam_q2, da_lam_k2, da_subln_g, w_in_odd, w_out_odd, ml_conv_w, ml_conv_b, ml_w_q, ml_w_k, ml_w_v, ml_b_i, ml_b_f, ml_gn_g, ml_skip, w_router, b_router, w_e1, w_e3, w_e2):
    nb, t, d = x_prompt.shape
    nbs = x_sample.shape[0]
    assert x_sample.shape[1] == 1
    depth = ln1_g.shape[0]
    wa = rg_conv_w.shape[2]
    hd = da_lam_q1.shape[1]
    nh = cache_k.shape[3]
    qw = nh * 2 * hd
    wc = ml_conv_w.shape[2]
    mh = ml_b_i.shape[1]
    dh = wc // mh
    n_exp = w_router.shape[1]
    past_len = page_table.shape[1] * cache_k.shape[2]
    alpha = (2.0 * depth) ** 0.25
    assert wa == qw, "column blocks of the even in-projection are addressed in units of wa"

    n_p = nb * t
    nt = n_p + nbs
    xp = x_prompt.reshape(n_p, d)
    xs = x_sample.reshape(nbs, d)

    tm_p = _tile(n_p, 512)
    tn = _tile(d, 1024)
    tc = _tile(t, 256)
    bq = _tile(t, 512)
    ml_chunk = _tile(t, 256)
    tm_ln = _tile(n_p, 256)
    bm = MOE_BLOCK_ROWS
    n_blocks = -(-(TOP_K * nt + n_exp * (bm - 1)) // bm)
    tile_rows = (tm_p,) * (n_p // tm_p) + (nbs,)

    cos_p, sin_p = _rope_tables(jnp.arange(t, dtype=I32), hd)
    cos_s, sin_s = _rope_tables(jnp.full((nbs,), past_len, I32), hd)
    w_router_t = w_router.T
    kscale = dh ** -0.5
    w_gate = w_in_odd[:, :, 2 * wc:]
    P = PRECISE_PASSES

    outs = {k: [] for k in ("kp", "vp", "ks", "vs", "hp", "hs", "rcp", "rcs",
                            "cp", "cs", "np", "ns", "mp", "ms", "mcp", "mcs")}

    for l in range(depth):
        pp = P if l < depth - 1 else LAST_LAYER_PASSES
        if l % 2 == 0:
            e = l // 2
            lam_init = 0.8 - 0.6 * math.exp(-0.3 * l)
            rg = (rg_conv_w, rg_conv_b, rg_w_r, rg_b_r, rg_w_i, rg_b_i, rg_lambda)
            lams = (da_lam_q1, da_lam_k1, da_lam_q2, da_lam_k2, da_subln_g)
            n_in = w_in_even.shape[2]
            u_p = _mm([xp], w_in_even, e, 0, n_in, tm_p, tn, pp)
            ya_p, h_p = _rglru_prompt(u_p, nb, t, wa, e, *rg, tc, pp)
            qr_p, kr_p = _rope(u_p, 2, 3, qw, cos_p, sin_p, hd, tm_p)
            o_p = _attn_prompt(qr_p, kr_p, u_p, (2 * wa + 2 * qw) // (2 * hd), nb, t, nh, hd, e, *lams,
                               lam_init, bq, bq, pp)
            mix_p = _mm([ya_p, o_p], w_out_even, e, 0, d, tm_p, tn, pp)
            v_p = u_p[:, 2 * wa + 2 * qw:]
            outs["kp"].append(kr_p.reshape(nb, t, nh, 2 * hd))
            outs["vp"].append(v_p.reshape(nb, t, nh, 2 * hd))
            outs["hp"].append(h_p)
            outs["rcp"].append(u_p[:, :wa].reshape(nb, t, wa)[:, t - (CONV_W - 1):])
            u_s = _mm([xs], w_in_even, e, 0, n_in, nbs, tn, P)
            buf = state_rglru_conv[e]
            ya_s, h_s = _rglru_step(u_s, wa, e, jnp.swapaxes(buf, 0, 1), state_rglru_h, *rg, P)
            qr_s, kr_s = _rope(u_s, 2, 3, qw, cos_s, sin_s, hd, nbs)
            v_s = u_s[:, 2 * wa + 2 * qw:]
            o_s = _attn_decode(page_table, e, qr_s, kr_s, v_s, cache_k, cache_v, nh, hd, *lams, lam_init)
            mix_s = _mm([ya_s, o_s], w_out_even, e, 0, d, nbs, tn, P)
            outs["ks"].append(kr_s.reshape(nbs, 1, nh, 2 * hd))
            outs["vs"].append(v_s.reshape(nbs, 1, nh, 2 * hd))
            outs["hs"].append(h_s)
            outs["rcs"].append(jnp.concatenate([buf[:, 1:], u_s[:, None, :wa]], axis=1))
        else:
            o = l // 2
            wqk = jnp.concatenate([_expand_block_diag(ml_w_q[o], V7X_LANES),
                                   _expand_block_diag(ml_w_k[o], V7X_LANES)], axis=2)
            wv = _expand_block_diag(ml_w_v[o], V7X_LANES)
            bias = jnp.concatenate([ml_b_i[o], ml_b_f[o]])
            u_p = _mm([xp], w_in_odd, o, 0, 2 * wc, tm_p, tn, pp)
            g_p = _mm([xp], w_gate, o, 0, 2 * mh, tm_p, 2 * mh, P)
            xc_p, q_p, k_p, v_p = _ml_pre_prompt(u_p, nb, t, wc, o, ml_conv_w, ml_conv_b, wqk, wv,
                                                 kscale, tc, pp)
            g_rows = jnp.swapaxes(g_p.reshape(nb, t, 2 * mh), 1, 2)
            pre_p, c_p, nn_p, m_p = _ml_scan_prompt(
                q_p, k_p, v_p, u_p, wc // dh, xc_p, g_rows, g_p, bias.reshape(1, -1), bias.reshape(-1, 1),
                o, ml_gn_g, ml_skip, nb, t, mh, dh, ml_chunk, pp)
            mix_p = _mm([pre_p], w_out_odd, o, 0, d, tm_p, tn, pp)
            outs["cp"].append(c_p)
            outs["np"].append(nn_p[:, :, 0])
            outs["mp"].append(m_p[:, :, 0, 0])
            outs["mcp"].append(u_p[:, :wc].reshape(nb, t, wc)[:, t - (CONV_W - 1):])
            u_s = _mm([xs], w_in_odd, o, 0, 2 * wc, nbs, tn, P)
            g_s = _mm([xs], w_gate, o, 0, 2 * mh, nbs, 2 * mh, P)
            buf = state_mlstm_conv[o]
            xc_s, q_s, k_s, v_s = _ml_pre_step(u_s, wc, o, jnp.swapaxes(buf, 0, 1), ml_conv_w,
                                               ml_conv_b, wqk, wv, kscale, P)
            pre_s, c_s, nn_s, m_s = _ml_step(
                q_s, k_s, v_s, u_s, wc // dh, xc_s, g_s[:, :mh] + ml_b_i[o], g_s[:, mh:] + ml_b_f[o],
                o, state_mlstm_m[o], state_mlstm_c, state_mlstm_n, ml_gn_g, ml_skip, mh, dh, P)
            mix_s = _mm([pre_s], w_out_odd, o, 0, d, nbs, tn, P)
            outs["cs"].append(c_s)
            outs["ns"].append(nn_s)
            outs["ms"].append(m_s)
            outs["mcs"].append(jnp.concatenate([buf[:, 1:], u_s[:, None, :wc]], axis=1))

        xp = _ln_add(xp, mix_p, ln1_g, ln1_b, l, alpha, tm_ln)
        xs = _ln_add(xs, mix_s, ln1_g, ln1_b, l, alpha, nbs)
        idx_p, gate_p, pos_p, cnt_p = _router(xp, w_router_t, b_router, tm_p, tm_p, P)
        xs_pad = jnp.pad(xs, ((0, ROUTER_SAMPLE_TILE - nbs), (0, 0)))
        idx_s, gate_s, pos_s, cnt_s = _router(xs_pad, w_router_t, b_router, ROUTER_SAMPLE_TILE, nbs, P)
        idx = jnp.concatenate([idx_p, idx_s[:, :nbs]], axis=1)
        pos = jnp.concatenate([pos_p, pos_s[:, :nbs]], axis=1)
        gates = jnp.concatenate([gate_p, gate_s[:, :nbs]], axis=1).T
        cnt = jnp.concatenate([cnt_p, cnt_s], axis=0)
        dest, blk_e, blk_n = _dispatch_plan(idx, pos, cnt, tile_rows, n_exp, bm, n_blocks)
        xb = _dispatch(dest, xp, 0, nt, n_blocks * bm, tm_ln)
        xb = _dispatch(dest, xs, n_p, nt, n_blocks * bm, nbs, xb_prev=xb)
        yb = _experts(xb, blk_e, blk_n, l, w_e1, w_e3, w_e2, bm, MOE_HIDDEN_SPLIT, pp)
        xp = _combine_ln(dest, xp, gates, yb, 0, ln2_g, ln2_b, l, alpha, tm_ln)
        xs = _combine_ln(dest, xs, gates, yb, n_p, ln2_g, ln2_b, l, alpha, nbs)

    st = lambda k: jnp.stack(outs[k])
    return (xp.reshape(nb, t, d), xs.reshape(nbs, 1, d),
            st("kp"), st("vp"), st("ks"), st("vs"), st("hp"), st("hs"), st("rcp"), st("rcs"),
            st("cp"), st("cs"), st("np"), st("ns"), st("mp"), st("ms"), st("mcp"), st("mcs"))
```
